```python
import jax, jax.numpy as jnp
from jax import lax
import numpy as np

D_MODEL = 1024
BATCH = 8
SEQ = 2048
DEPTH = 1
DEC_BATCH = 32
DEC_SEQ = 1
PAST_LEN = 16384
PAGE_SIZE = 128

N_MEM = 256
SB_HEADS = 8
SB_HEAD_DIM = 64
SB_WIDTH = SB_HEADS * SB_HEAD_DIM
SB_BLOCK = 128
SB_SCALE = SB_HEAD_DIM ** -0.5
SB_BIAS_INIT = -6.0
MEM_HEADS = 4
MEM_HEAD_DIM = 128
MEM_WIDTH = MEM_HEADS * MEM_HEAD_DIM
MEM_SCALE = MEM_HEAD_DIM ** -0.5
RW_HEADS = 8
RW_HEAD_DIM = 64
RW_WIDTH = RW_HEADS * RW_HEAD_DIM
DECAY_LORA = 64
AAA_LORA = 64
GATE_LORA = 128
RW_SPLITS = (RW_WIDTH, RW_WIDTH, RW_WIDTH, DECAY_LORA, AAA_LORA, GATE_LORA)
RW_SHIFT_WIDTH = sum(RW_SPLITS)
RW_GN_EPS = 64e-5
N_EXPERTS = 32
TOP_K = 4
D_FF = D_MODEL
SWIGLU_ALPHA = 1.702
SWIGLU_LIMIT = 7.0
MOE_BLOCK = 128
DN_ALPHA = (2 * DEPTH) ** 0.25
DN_BETA = (8 * DEPTH) ** -0.25
LN_EPS = 1e-5
IN_SPLITS = (SB_WIDTH, SB_WIDTH, SB_WIDTH, MEM_WIDTH, RW_SHIFT_WIDTH, D_MODEL, D_MODEL, D_MODEL)
IN_WIDTH = sum(IN_SPLITS)

kernel_name = 'stickbreak_rwkv7_memxattn_moe_deepnorm_step'


def _split(x, sizes):
    return jnp.split(x, np.cumsum(sizes)[:-1].tolist(), axis=-1)


def _layer_norm(x, g, b):
    xf = x.astype(jnp.float32)
    mu = xf.mean(-1, keepdims=True)
    var = jnp.square(xf - mu).mean(-1, keepdims=True)
    return ((xf - mu) * lax.rsqrt(var + LN_EPS) * g + b).astype(x.dtype)


def _sb_logits(q, k, bias):
    z = jnp.einsum('bqhd,bkhd->bhqk', q, k, preferred_element_type=jnp.float32) * SB_SCALE
    return z + bias.astype(jnp.float32)[None, :, None, None]


def _sb_weights(z, causal):
    log_keep = jnp.where(causal, jax.nn.log_sigmoid(-z), 0.0)
    log_survive = lax.cumsum(log_keep, axis=z.ndim - 1, reverse=True) - log_keep
    return jnp.where(causal, jnp.exp(jax.nn.log_sigmoid(z) + log_survive), 0.0)


def _sb_prompt(q, k, v, bias):
    b, s = q.shape[:2]
    n_blk = s // SB_BLOCK
    q_blocks = jnp.moveaxis(q.reshape(b, n_blk, SB_BLOCK, SB_HEADS, SB_HEAD_DIM), 1, 0)
    k_pos = jnp.arange(s)

    def one_block(args):
        q_blk, blk = args
        q_pos = blk * SB_BLOCK + jnp.arange(SB_BLOCK)
        w = _sb_weights(_sb_logits(q_blk, k, bias), k_pos[None, :] < q_pos[:, None])
        return jnp.einsum('bhqk,bkhd->bqhd', w.astype(v.dtype), v)

    o = lax.map(one_block, (q_blocks, jnp.arange(n_blk)))
    return jnp.moveaxis(o, 0, 1).reshape(b, s, SB_WIDTH)


def _sb_sample(q, k_new, v_new, bias, k_pool, v_pool, layer, page_table):
    b, s = q.shape[:2]
    past = page_table.shape[1] * PAGE_SIZE
    k_past = k_pool[layer][page_table].reshape(b, past, SB_HEADS, SB_HEAD_DIM)
    v_past = v_pool[layer][page_table].reshape(b, past, SB_HEADS, SB_HEAD_DIM)
    z = jnp.concatenate([_sb_logits(q, k_past, bias), _sb_logits(q, k_new, bias)], axis=-1)
    q_pos = past + jnp.arange(s)
    k_pos = jnp.arange(past + s)
    w = _sb_weights(z, k_pos[None, :] < q_pos[:, None])
    o = (jnp.einsum('bhqk,bkhd->bqhd', w[..., :past].astype(v_past.dtype), v_past)
         + jnp.einsum('bhqk,bkhd->bqhd', w[..., past:].astype(v_new.dtype), v_new))
    return o.reshape(b, s, SB_WIDTH).astype(q.dtype)


def _mem_attend(q, mem_k, mem_v):
    s = jnp.einsum('bshd,bmhd->bhsm', q, mem_k, preferred_element_type=jnp.float32) * MEM_SCALE
    p = jax.nn.softmax(s, axis=-1)
    return jnp.einsum('bhsm,bmhd->bshd', p.astype(mem_v.dtype), mem_v)


def _rwkv7(p_rw, shift0, wkv0, mu_rw, w_decay0, w_decay2, w_aaa0, w_aaa2, w_gate2,
           rw_k_k, rw_k_a, rw_r_k, rw_gn_g, rw_gn_b):
    f32 = jnp.float32
    b, s, _ = p_rw.shape
    prev = jnp.concatenate([shift0[:, None, :].astype(p_rw.dtype), p_rw[:, :-1]], axis=1)
    xs = p_rw + mu_rw * (prev - p_rw)
    r, k, v, w_lo, a_lo, g_lo = _split(xs, RW_SPLITS)
    w_raw = (w_decay0 + jnp.tanh(w_lo) @ w_decay2).astype(f32)
    decay = jnp.exp(-jnp.exp(-jax.nn.softplus(-w_raw) - 0.5))
    a = jax.nn.sigmoid((w_aaa0 + a_lo @ w_aaa2).astype(f32))
    g = jax.nn.sigmoid(g_lo) @ w_gate2
    heads = lambda t: t.reshape(b, s, RW_HEADS, RW_HEAD_DIM)
    kk = heads(k.astype(f32) * rw_k_k)
    kk = kk / jnp.maximum(jnp.sqrt(jnp.sum(kk * kk, axis=-1, keepdims=True)), 1e-12)
    k_mod = k.astype(f32) * (1.0 + (a - 1.0) * rw_k_a)
    rh, kh, vh, ah, dh = heads(r.astype(f32)), heads(k_mod), heads(v.astype(f32)), heads(a), heads(decay)

    def step(state, inp):
        r_t, d_t, k_t, v_t, kk_t, a_t = inp
        s_a = jnp.einsum('bhvk,bhk->bhv', state, -kk_t)
        state = (state * d_t[:, :, None, :] + s_a[..., None] * (kk_t * a_t)[:, :, None, :]
                 + v_t[..., None] * k_t[:, :, None, :])
        return state, jnp.einsum('bhvk,bhk->bhv', state, r_t)

    seq_first = lambda t: jnp.moveaxis(t, 1, 0)
    wkv, y = lax.scan(step, wkv0.astype(f32), (seq_first(rh), seq_first(dh), seq_first(kh),
                                               seq_first(vh), seq_first(kk), seq_first(ah)))
    y = jnp.moveaxis(y, 0, 1)
    mu = y.mean(-1, keepdims=True)
    var = jnp.square(y - mu).mean(-1, keepdims=True)
    y = ((y - mu) * lax.rsqrt(var + RW_GN_EPS)).reshape(b, s, RW_WIDTH) * rw_gn_g + rw_gn_b
    bonus = jnp.sum(rh * kh * rw_r_k, axis=-1, keepdims=True) * vh
    y = (y + bonus.reshape(b, s, RW_WIDTH)) * g
    return y.astype(p_rw.dtype), p_rw[:, -1], wkv


def _moe(x, w_router, b_router, w_gate_up, b_gate_up, w_down, b_down):
    lead = x.shape[:-1]
    xf = x.reshape(-1, D_MODEL)
    t = xf.shape[0]
    logits = jnp.dot(xf, w_router, preferred_element_type=jnp.float32) + b_router
    top_logit, top_e = lax.top_k(logits, TOP_K)
    top_p = jax.nn.softmax(top_logit, axis=-1)
    n_pairs = t * TOP_K
    n_blocks = -(-n_pairs // MOE_BLOCK) + N_EXPERTS
    pair_e = top_e.reshape(-1)
    order = jnp.argsort(pair_e)
    sorted_e = pair_e[order]
    sorted_tok = order // TOP_K
    counts = jnp.bincount(pair_e, length=N_EXPERTS)
    padded = (counts + MOE_BLOCK - 1) // MOE_BLOCK * MOE_BLOCK
    pad_end = jnp.cumsum(padded)
    rank = jnp.arange(n_pairs) - (jnp.cumsum(counts) - counts)[sorted_e]
    dest = (pad_end - padded)[sorted_e] + rank
    row_tok = jnp.zeros((n_blocks * MOE_BLOCK,), jnp.int32).at[dest].set(sorted_tok)
    row_w = jnp.zeros((n_blocks * MOE_BLOCK,), jnp.float32).at[dest].set(top_p.reshape(-1)[order])
    block_e = jnp.minimum(jnp.searchsorted(pad_end, jnp.arange(n_blocks) * MOE_BLOCK, side='right'),
                          N_EXPERTS - 1)
    xb = xf[row_tok].reshape(n_blocks, MOE_BLOCK, D_MODEL)

    def expert_block(args):
        xe, e = args
        gu = xe @ w_gate_up[e] + b_gate_up[e]
        glu = jnp.minimum(gu[:, ::2], SWIGLU_LIMIT)
        lin = jnp.clip(gu[:, 1::2], -SWIGLU_LIMIT, SWIGLU_LIMIT)
        h = glu * jax.nn.sigmoid(SWIGLU_ALPHA * glu) * (lin + 1.0)
        return h @ w_down[e] + b_down[e]

    yb = lax.map(expert_block, (xb, block_e)).reshape(-1, D_MODEL)
    y = jnp.zeros((t, D_MODEL), jnp.float32).at[row_tok].add(yb.astype(jnp.float32) * row_w[:, None])
    return y.astype(x.dtype).reshape(*lead, D_MODEL)


def _layer(x, sb_mix, mem_k, mem_v, shift0, wkv0, w_in, sb_bias, mu_rw, w_decay0, w_decay2, w_aaa0,
           w_aaa2, w_gate2, rw_k_k, rw_k_a, rw_r_k, rw_gn_g, rw_gn_b, w_sb_o, w_mem_o, w_rw_o, w_out,
           ln1_g, ln1_b, w_router, b_router, w_gate_up, b_gate_up, w_down, b_down, ln2_g, ln2_b):
    b, s, _ = x.shape
    q_sb, k_sb, v_sb, q_mem, p_rw, g_sb, g_mem, g_rw = _split(x @ w_in, IN_SPLITS)
    sb_heads = lambda t: t.reshape(b, s, SB_HEADS, SB_HEAD_DIM)
    k_sb, v_sb = sb_heads(k_sb), sb_heads(v_sb)
    o_sb = sb_mix(sb_heads(q_sb), k_sb, v_sb, sb_bias)
    o_mem = _mem_attend(q_mem.reshape(b, s, MEM_HEADS, MEM_HEAD_DIM), mem_k, mem_v).reshape(b, s, MEM_WIDTH)
    o_rw, shift, wkv = _rwkv7(p_rw, shift0, wkv0, mu_rw, w_decay0, w_decay2, w_aaa0, w_aaa2, w_gate2,
                              rw_k_k, rw_k_a, rw_r_k, rw_gn_g, rw_gn_b)
    merged = (jax.nn.sigmoid(g_sb) * (o_sb @ w_sb_o) + jax.nn.sigmoid(g_mem) * (o_mem @ w_mem_o)
              + jax.nn.sigmoid(g_rw) * (o_rw @ w_rw_o))
    h = _layer_norm(DN_ALPHA * x + merged @ w_out, ln1_g, ln1_b)
    y = _layer_norm(DN_ALPHA * h + _moe(h, w_router, b_router, w_gate_up, b_gate_up, w_down, b_down),
                    ln2_g, ln2_b)
    return y, k_sb, v_sb, shift, wkv


def setup_inputs(seed: int = 0) -> dict:
    key = jax.random.key(seed)
    keys = iter(jax.random.split(key, 48))

    def nrm(shape, scale):
        return scale * jax.random.normal(next(keys), shape, jnp.float32)

    n_pages = PAST_LEN // PAGE_SIZE
    n_used = DEC_BATCH * n_pages
    n_pool = n_used + max(n_used // 4, 1)
    page_table = jax.random.permutation(next(keys), n_pool)[:n_used].reshape(DEC_BATCH, n_pages).astype(jnp.int32)
    L = DEPTH
    decay_base = jnp.linspace(-6.0, -1.0, RW_WIDTH, dtype=jnp.float32)
    return {
        'x_prompt': nrm((BATCH, SEQ, D_MODEL), 1.0),
        'x_sample': nrm((DEC_BATCH, DEC_SEQ, D_MODEL), 1.0),
        'cache_sb_k': nrm((L, n_pool, PAGE_SIZE, SB_HEADS, SB_HEAD_DIM), 1.0),
        'cache_sb_v': nrm((L, n_pool, PAGE_SIZE, SB_HEADS, SB_HEAD_DIM), 1.0),
        'cache_mem_k': nrm((L, DEC_BATCH, N_MEM, MEM_HEADS, MEM_HEAD_DIM), 1.0),
        'cache_mem_v': nrm((L, DEC_BATCH, N_MEM, MEM_HEADS, MEM_HEAD_DIM), 1.0),
        'state_rw_shift': nrm((L, DEC_BATCH, RW_SHIFT_WIDTH), 1.0),
        'state_rw_wkv': nrm((L, DEC_BATCH, RW_HEADS, RW_HEAD_DIM, RW_HEAD_DIM), 1.0),
        'page_table': page_table,
        'mem_prompt': nrm((BATCH, N_MEM, D_MODEL), 1.0),
        'w_in': nrm((L, D_MODEL, IN_WIDTH), D_MODEL ** -0.5),
        'sb_bias': SB_BIAS_INIT + nrm((L, SB_HEADS), 0.1),
        'mu_rw': jax.random.uniform(next(keys), (L, RW_SHIFT_WIDTH), jnp.float32),
        'w_decay0': decay_base + nrm((L, RW_WIDTH), 0.1),
        'w_decay2': nrm((L, DECAY_LORA, RW_WIDTH), 0.1 * DECAY_LORA ** -0.5),
        'w_aaa0': nrm((L, RW_WIDTH), 0.5),
        'w_aaa2': nrm((L, AAA_LORA, RW_WIDTH), AAA_LORA ** -0.5),
        'w_gate2': nrm((L, GATE_LORA, RW_WIDTH), GATE_LORA ** -0.5),
        'rw_k_k': 0.85 + nrm((L, RW_WIDTH), 0.05),
        'rw_k_a': 1.0 + nrm((L, RW_WIDTH), 0.05),
        'rw_r_k': nrm((L, RW_HEADS, RW_HEAD_DIM), 0.1),
        'rw_gn_g': 1.0 + nrm((L, RW_WIDTH), 0.02),
        'rw_gn_b': nrm((L, RW_WIDTH), 0.02),
        'w_mem_kv': nrm((L, D_MODEL, 2 * MEM_WIDTH), D_MODEL ** -0.5),
        'w_sb_o': nrm((L, SB_WIDTH, D_MODEL), SB_WIDTH ** -0.5),
        'w_mem_o': nrm((L, MEM_WIDTH, D_MODEL), MEM_WIDTH ** -0.5),
        'w_rw_o': nrm((L, RW_WIDTH, D_MODEL), RW_WIDTH ** -0.5),
        'w_out': nrm((L, D_MODEL, D_MODEL), DN_BETA * D_MODEL ** -0.5),
        'ln1_g': 1.0 + nrm((L, D_MODEL), 0.02),
        'ln1_b': nrm((L, D_MODEL), 0.02),
        'w_router': nrm((L, D_MODEL, N_EXPERTS), D_MODEL ** -0.5),
        'b_router': nrm((L, N_EXPERTS), 0.01),
        'w_gate_up': nrm((L, N_EXPERTS, D_MODEL, 2 * D_FF), D_MODEL ** -0.5),
        'b_gate_up': nrm((L, N_EXPERTS, 2 * D_FF), 0.02),
        'w_down': nrm((L, N_EXPERTS, D_FF, D_MODEL), DN_BETA * D_FF ** -0.5),
        'b_down': nrm((L, N_EXPERTS, D_MODEL), 0.02),
        'ln2_g': 1.0 + nrm((L, D_MODEL), 0.02),
        'ln2_b': nrm((L, D_MODEL), 0.02),
    }


def reference(x_prompt, x_sample, cache_sb_k, cache_sb_v, cache_mem_k, cache_mem_v, state_rw_shift,
              state_rw_wkv, page_table, mem_prompt, w_in, sb_bias, mu_rw, w_decay0, w_decay2, w_aaa0,
              w_aaa2, w_gate2, rw_k_k, rw_k_a, rw_r_k, rw_gn_g, rw_gn_b, w_mem_kv, w_sb_o, w_mem_o,
              w_rw_o, w_out, ln1_g, ln1_b, w_router, b_router, w_gate_up, b_gate_up, w_down, b_down,
              ln2_g, ln2_b):
    layer_weights = (w_in, sb_bias, mu_rw, w_decay0, w_decay2, w_aaa0, w_aaa2, w_gate2, rw_k_k, rw_k_a,
                     rw_r_k, rw_gn_g, rw_gn_b, w_sb_o, w_mem_o, w_rw_o, w_out, ln1_g, ln1_b, w_router,
                     b_router, w_gate_up, b_gate_up, w_down, b_down, ln2_g, ln2_b)
    bp = x_prompt.shape[0]
    xp, xs = x_prompt, x_sample
    kp, vp, mkp, mvp, shp, wkp, ks, vs, shs, wks = ([] for _ in range(10))
    for l in range(DEPTH):
        lw = [p[l] for p in layer_weights]
        mem_k, mem_v = _split(mem_prompt @ w_mem_kv[l], (MEM_WIDTH, MEM_WIDTH))
        mem_k = mem_k.reshape(bp, N_MEM, MEM_HEADS, MEM_HEAD_DIM)
        mem_v = mem_v.reshape(bp, N_MEM, MEM_HEADS, MEM_HEAD_DIM)
        xp, k_p, v_p, sh_p, wkv_p = _layer(
            xp, _sb_prompt, mem_k, mem_v, jnp.zeros((bp, RW_SHIFT_WIDTH), xp.dtype),
            jnp.zeros((bp, RW_HEADS, RW_HEAD_DIM, RW_HEAD_DIM), jnp.float32), *lw)
        sb_past = lambda q, k, v, bias, l=l: _sb_sample(q, k, v, bias, cache_sb_k, cache_sb_v, l, page_table)
        xs, k_s, v_s, sh_s, wkv_s = _layer(
            xs, sb_past, cache_mem_k[l], cache_mem_v[l], state_rw_shift[l], state_rw_wkv[l], *lw)
        kp.append(k_p); vp.append(v_p); mkp.append(mem_k); mvp.append(mem_v)
        shp.append(sh_p); wkp.append(wkv_p)
        ks.append(k_s); vs.append(v_s); shs.append(sh_s); wks.append(wkv_s)
    y_prompt, y_sample = xp, xs
    sb_k_prompt, sb_v_prompt = jnp.stack(kp), jnp.stack(vp)
    mem_k_prompt, mem_v_prompt = jnp.stack(mkp), jnp.stack(mvp)
    rw_shift_prompt, rw_wkv_prompt = jnp.stack(shp), jnp.stack(wkp)
    sb_k_sample, sb_v_sample = jnp.stack(ks), jnp.stack(vs)
    rw_shift_sample, rw_wkv_sample = jnp.stack(shs), jnp.stack(wks)
    return (y_prompt, y_sample, sb_k_prompt, sb_v_prompt, mem_k_prompt, mem_v_prompt,
            rw_shift_prompt, rw_wkv_prompt, sb_k_sample, sb_v_sample, rw_shift_sample, rw_wkv_sample)
```

```python
import functools

import jax
import jax.numpy as jnp
import numpy as np
from jax import lax
from jax.experimental import pallas as pl
from jax.experimental.pallas import tpu as pltpu

D_MODEL = 1024
PAGE_SIZE = 128
N_MEM = 256
SB_HEADS = 8
SB_HEAD_DIM = 64
SB_WIDTH = SB_HEADS * SB_HEAD_DIM
SB_SCALE = SB_HEAD_DIM ** -0.5
MEM_HEADS = 4
MEM_HEAD_DIM = 128
MEM_WIDTH = MEM_HEADS * MEM_HEAD_DIM
MEM_SCALE = MEM_HEAD_DIM ** -0.5
RW_HEADS = 8
RW_HEAD_DIM = 64
RW_WIDTH = RW_HEADS * RW_HEAD_DIM
DECAY_LORA = 64
AAA_LORA = 64
GATE_LORA = 128
RW_SPLITS = (RW_WIDTH, RW_WIDTH, RW_WIDTH, DECAY_LORA, AAA_LORA, GATE_LORA)
RW_SHIFT_WIDTH = sum(RW_SPLITS)
RW_GN_EPS = 64e-5
N_EXPERTS = 32
TOP_K = 4
D_FF = D_MODEL
SWIGLU_ALPHA = 1.702
SWIGLU_LIMIT = 7.0
DEPTH = 1
DN_ALPHA = (2 * DEPTH) ** 0.25
LN_EPS = 1e-5
IN_SPLITS = (SB_WIDTH, SB_WIDTH, SB_WIDTH, MEM_WIDTH, RW_SHIFT_WIDTH, D_MODEL, D_MODEL, D_MODEL)
IN_WIDTH = sum(IN_SPLITS)

LANES = 128
MXU_DIM = 256
VMEM_LIMIT_BYTES = 48 * 1024 * 1024

SB_TILE = MXU_DIM
PROJ_TILE_M = 256
MOE_TILE_M = 256

f32 = jnp.float32
bf16 = jnp.bfloat16


def _split(x, sizes):
    return jnp.split(x, np.cumsum(sizes)[:-1].tolist(), axis=-1)


def _proj_kernel(x_ref, w_ref, *o_refs, splits):
    xb = x_ref[...].astype(bf16)
    off = 0
    for o_ref, n in zip(o_refs, splits):
        o_ref[...] = jnp.dot(xb, w_ref[:, off:off + n], preferred_element_type=f32)
        off += n


def _proj(x, w_bf16, splits, tile_m):
    t, d = x.shape
    n = w_bf16.shape[1]
    assert t % tile_m == 0 and sum(splits) == n
    return pl.pallas_call(
        functools.partial(_proj_kernel, splits=splits),
        grid=(t // tile_m,),
        in_specs=[
            pl.BlockSpec((tile_m, d), lambda i: (i, 0)),
            pl.BlockSpec((d, n), lambda i: (0, 0), pipeline_mode=pl.Buffered(1)),
        ],
        out_specs=[pl.BlockSpec((tile_m, s), lambda i: (i, 0)) for s in splits],
        out_shape=[jax.ShapeDtypeStruct((t, s), f32) for s in splits],
        compiler_params=pltpu.CompilerParams(
            dimension_semantics=("arbitrary",), vmem_limit_bytes=VMEM_LIMIT_BYTES),
        name="proj_in",
    )(x, w_bf16)


def _sb_prompt_kernel(bias_ref, q_ref, k_ref, v_ref, o_ref, *, tile):
    hp = pl.program_id(1)
    qi = pl.program_id(2)
    row = lax.broadcasted_iota(jnp.int32, (tile, tile), 0)
    col = lax.broadcasted_iota(jnp.int32, (tile, tile), 1)
    later = (row > col).astype(bf16)
    causal = col < row
    q_all = q_ref[...] * SB_SCALE
    outs = []
    for hh in range(LANES // SB_HEAD_DIM):
        lo, hi = hh * SB_HEAD_DIM, (hh + 1) * SB_HEAD_DIM
        qh = q_all[:, lo:hi].astype(bf16)
        bias = bias_ref[hp * (LANES // SB_HEAD_DIM) + hh]

        def key_tile(j, carry, acc, masked, qh=qh, bias=bias, lo=lo, hi=hi):
            start = pl.multiple_of(j * tile, tile)
            kh = k_ref[pl.ds(start, tile), lo:hi].astype(bf16)
            vh = v_ref[pl.ds(start, tile), lo:hi].astype(bf16)
            z = lax.dot_general(qh, kh, (((1,), (1,)), ((), ())), preferred_element_type=f32) + bias
            softplus = jnp.maximum(z, 0.0) + jnp.log(1.0 + jnp.exp(-jnp.abs(z)))
            log_keep = -softplus
            if masked:
                log_keep = jnp.where(causal, log_keep, 0.0)
            inner = jnp.dot(log_keep.astype(bf16), later, preferred_element_type=f32)
            a = jnp.exp(z - softplus + carry + inner)
            if masked:
                a = jnp.where(causal, a, 0.0)
            acc = acc + jnp.dot(a.astype(bf16), vh, preferred_element_type=f32)
            carry = carry + jnp.sum(log_keep, axis=1, keepdims=True)
            return carry, acc

        carry = jnp.zeros((tile, 1), f32)
        acc = jnp.zeros((tile, SB_HEAD_DIM), f32)
        carry, acc = key_tile(qi, carry, acc, True)
        carry, acc = lax.fori_loop(
            0, qi, lambda s, c, key_tile=key_tile: key_tile(qi - 1 - s, c[0], c[1], False), (carry, acc))
        outs.append(acc)
    o_ref[...] = jnp.concatenate(outs, axis=1)


def _sb_prompt(q, k, v, bias, tile=SB_TILE):
    b, s, w = q.shape
    assert s % tile == 0 and w % LANES == 0
    blk = lambda bb, hp, qi: (bb, qi, hp)
    full = lambda bb, hp, qi: (bb, 0, hp)
    return pl.pallas_call(
        functools.partial(_sb_prompt_kernel, tile=tile),
        grid=(b, w // LANES, s // tile),
        in_specs=[
            pl.BlockSpec(memory_space=pltpu.SMEM),
            pl.BlockSpec((None, tile, LANES), blk),
            pl.BlockSpec((None, s, LANES), full),
            pl.BlockSpec((None, s, LANES), full),
        ],
        out_specs=pl.BlockSpec((None, tile, LANES), blk),
        out_shape=jax.ShapeDtypeStruct((b, s, w), f32),
        compiler_params=pltpu.CompilerParams(
            dimension_semantics=("arbitrary", "arbitrary", "arbitrary"),
            vmem_limit_bytes=VMEM_LIMIT_BYTES),
        name="sb_prompt",
    )(bias, q, k, v)


def _moe_kernel(be_ref, x_ref, wg_ref, wl_ref, bg_ref, bl_ref, wd_ref, bd_ref, rw_ref, o_ref):
    del be_ref
    x = x_ref[...]
    g = jnp.dot(x, wg_ref[...], preferred_element_type=f32) + bg_ref[...]
    l = jnp.dot(x, wl_ref[...], preferred_element_type=f32) + bl_ref[...]
    glu = jnp.minimum(g, SWIGLU_LIMIT)
    lin = jnp.clip(l, -SWIGLU_LIMIT, SWIGLU_LIMIT)
    h = glu * (1.0 / (1.0 + jnp.exp(-SWIGLU_ALPHA * glu))) * (lin + 1.0)
    y = jnp.dot(h.astype(bf16), wd_ref[...], preferred_element_type=f32) + bd_ref[...]
    o_ref[...] = y * rw_ref[...]


def _moe_experts(xb, block_e, wg, wl, bg, bl, wd, bd, row_w, tile_m):
    n_rows, d = xb.shape
    n_blocks = n_rows // tile_m
    e, _, ff = wg.shape
    row = lambda i, be: (i, 0)
    wsel = lambda i, be: (be[i], 0, 0)
    grid_spec = pltpu.PrefetchScalarGridSpec(
        num_scalar_prefetch=1,
        grid=(n_blocks,),
        in_specs=[
            pl.BlockSpec((tile_m, d), row),
            pl.BlockSpec((None, d, ff), wsel),
            pl.BlockSpec((None, d, ff), wsel),
            pl.BlockSpec((None, 1, ff), wsel),
            pl.BlockSpec((None, 1, ff), wsel),
            pl.BlockSpec((None, ff, d), wsel),
            pl.BlockSpec((None, 1, d), wsel),
            pl.BlockSpec((tile_m, 1), row),
        ],
        out_specs=pl.BlockSpec((tile_m, d), row),
    )
    return pl.pallas_call(
        _moe_kernel,
        grid_spec=grid_spec,
        out_shape=jax.ShapeDtypeStruct((n_rows, d), f32),
        compiler_params=pltpu.CompilerParams(
            dimension_semantics=("arbitrary",), vmem_limit_bytes=VMEM_LIMIT_BYTES),
        name="moe_experts",
    )(block_e, xb, wg, wl, bg, bl, wd, bd, row_w)


def _moe(x, w_router, b_router, w_gate_up, b_gate_up, w_down, b_down, tile_m=MOE_TILE_M):
    t = x.shape[0]
    logits = jnp.dot(x, w_router, preferred_element_type=f32, precision=lax.Precision.HIGHEST) + b_router
    top_logit, top_e = lax.top_k(logits, TOP_K)
    top_p = jax.nn.softmax(top_logit, axis=-1)
    n_pairs = t * TOP_K
    n_blocks = -(-n_pairs // tile_m) + N_EXPERTS
    pair_e = top_e.reshape(-1)
    onehot = (pair_e[:, None] == jnp.arange(N_EXPERTS, dtype=pair_e.dtype)[None, :]).astype(jnp.int32)
    counts = onehot.sum(0)
    rank = jnp.take_along_axis(jnp.cumsum(onehot, axis=0) - onehot, pair_e[:, None], axis=1)[:, 0]
    padded = (counts + tile_m - 1) // tile_m * tile_m
    pad_end = jnp.cumsum(padded)
    dest = (pad_end - padded)[pair_e] + rank
    n_rows = n_blocks * tile_m
    row_tok = jnp.zeros((n_rows,), jnp.int32).at[dest].set(jnp.arange(n_pairs, dtype=jnp.int32) // TOP_K)
    row_w = jnp.zeros((n_rows,), f32).at[dest].set(top_p.reshape(-1))
    block_e = jnp.minimum(
        jnp.searchsorted(pad_end, jnp.arange(n_blocks, dtype=jnp.int32) * tile_m, side='right'),
        N_EXPERTS - 1).astype(jnp.int32)
    xb = x.astype(bf16)[row_tok]
    wg = w_gate_up[:, :, 0::2].astype(bf16)
    wl = w_gate_up[:, :, 1::2].astype(bf16)
    bg = b_gate_up[:, None, 0::2]
    bl = b_gate_up[:, None, 1::2]
    yb = _moe_experts(xb, block_e, wg, wl, bg, bl, w_down.astype(bf16), b_down[:, None, :],
                      row_w[:, None], tile_m)
    return yb[dest.reshape(t, TOP_K)].sum(axis=1)


def _layer_norm(x, g, b):
    mu = x.mean(-1, keepdims=True)
    var = jnp.square(x - mu).mean(-1, keepdims=True)
    return (x - mu) * lax.rsqrt(var + LN_EPS) * g + b


def _sb_logits(q, k, bias):
    z = jnp.einsum('bqhd,bkhd->bhqk', q, k, preferred_element_type=f32) * SB_SCALE
    return z + bias.astype(f32)[None, :, None, None]


def _sb_weights(z, causal):
    log_keep = jnp.where(causal, jax.nn.log_sigmoid(-z), 0.0)
    log_survive = lax.cumsum(log_keep, axis=z.ndim - 1, reverse=True) - log_keep
    return jnp.where(causal, jnp.exp(jax.nn.log_sigmoid(z) + log_survive), 0.0)


def _sb_sample(q, k_new, v_new, bias, k_pool, v_pool, page_table):
    b, s = q.shape[:2]
    past = page_table.shape[1] * PAGE_SIZE
    k_past = k_pool[page_table].reshape(b, past, SB_HEADS, SB_HEAD_DIM)
    v_past = v_pool[page_table].reshape(b, past, SB_HEADS, SB_HEAD_DIM)
    z = jnp.concatenate([_sb_logits(q, k_past, bias), _sb_logits(q, k_new, bias)], axis=-1)
    q_pos = past + jnp.arange(s)
    k_pos = jnp.arange(past + s)
    w = _sb_weights(z, k_pos[None, :] < q_pos[:, None])
    o = (jnp.einsum('bhqk,bkhd->bqhd', w[..., :past], v_past)
         + jnp.einsum('bhqk,bkhd->bqhd', w[..., past:], v_new))
    return o.reshape(b, s, SB_WIDTH)


def _mem_attend(q, mem_k, mem_v):
    s = jnp.einsum('bshd,bmhd->bhsm', q, mem_k, preferred_element_type=f32) * MEM_SCALE
    p = jax.nn.softmax(s, axis=-1)
    return jnp.einsum('bhsm,bmhd->bshd', p, mem_v)


def _rwkv7(p_rw, shift0, wkv0, mu_rw, w_decay0, w_decay2, w_aaa0, w_aaa2, w_gate2,
           rw_k_k, rw_k_a, rw_r_k, rw_gn_g, rw_gn_b):
    b, s, _ = p_rw.shape
    prev = jnp.concatenate([shift0[:, None, :], p_rw[:, :-1]], axis=1)
    xs = p_rw + mu_rw * (prev - p_rw)
    r, k, v, w_lo, a_lo, g_lo = _split(xs, RW_SPLITS)
    w_raw = w_decay0 + jnp.tanh(w_lo) @ w_decay2
    decay = jnp.exp(-jnp.exp(-jax.nn.softplus(-w_raw) - 0.5))
    a = jax.nn.sigmoid(w_aaa0 + a_lo @ w_aaa2)
    g = jax.nn.sigmoid(g_lo) @ w_gate2
    heads = lambda t: t.reshape(b, s, RW_HEADS, RW_HEAD_DIM)
    kk = heads(k * rw_k_k)
    kk = kk / jnp.maximum(jnp.sqrt(jnp.sum(kk * kk, axis=-1, keepdims=True)), 1e-12)
    k_mod = k * (1.0 + (a - 1.0) * rw_k_a)
    rh, kh, vh, ah, dh = heads(r), heads(k_mod), heads(v), heads(a), heads(decay)

    def step(state, inp):
        r_t, d_t, k_t, v_t, kk_t, a_t = inp
        s_a = jnp.einsum('bhvk,bhk->bhv', state, -kk_t)
        state = (state * d_t[:, :, None, :] + s_a[..., None] * (kk_t * a_t)[:, :, None, :]
                 + v_t[..., None] * k_t[:, :, None, :])
        return state, jnp.einsum('bhvk,bhk->bhv', state, r_t)

    seq_first = lambda t: jnp.moveaxis(t, 1, 0)
    wkv, y = lax.scan(step, wkv0, (seq_first(rh), seq_first(dh), seq_first(kh),
                                   seq_first(vh), seq_first(kk), seq_first(ah)))
    y = jnp.moveaxis(y, 0, 1)
    mu = y.mean(-1, keepdims=True)
    var = jnp.square(y - mu).mean(-1, keepdims=True)
    y = ((y - mu) * lax.rsqrt(var + RW_GN_EPS)).reshape(b, s, RW_WIDTH) * rw_gn_g + rw_gn_b
    bonus = jnp.sum(rh * kh * rw_r_k, axis=-1, keepdims=True) * vh
    y = (y + bonus.reshape(b, s, RW_WIDTH)) * g
    return y, p_rw[:, -1], wkv


def _mix(x, parts, o_sb, mem_k, mem_v, shift0, wkv0, lw):
    b, s, _ = x.shape
    q_mem, p_rw, g_sb, g_mem, g_rw = parts
    o_mem = _mem_attend(q_mem.reshape(b, s, MEM_HEADS, MEM_HEAD_DIM), mem_k, mem_v).reshape(b, s, MEM_WIDTH)
    o_rw, shift, wkv = _rwkv7(p_rw, shift0, wkv0, lw['mu_rw'], lw['w_decay0'], lw['w_decay2'],
                              lw['w_aaa0'], lw['w_aaa2'], lw['w_gate2'], lw['rw_k_k'], lw['rw_k_a'],
                              lw['rw_r_k'], lw['rw_gn_g'], lw['rw_gn_b'])
    merged = (jax.nn.sigmoid(g_sb) * (o_sb @ lw['w_sb_o']) + jax.nn.sigmoid(g_mem) * (o_mem @ lw['w_mem_o'])
              + jax.nn.sigmoid(g_rw) * (o_rw @ lw['w_rw_o']))
    h = _layer_norm(DN_ALPHA * x + merged @ lw['w_out'], lw['ln1_g'], lw['ln1_b'])
    return h, shift, wkv


def kernel(x_prompt, x_sample, cache_sb_k, cache_sb_v, cache_mem_k, cache_mem_v, state_rw_shift,
           state_rw_wkv, page_table, mem_prompt, w_in, sb_bias, mu_rw, w_decay0, w_decay2, w_aaa0,
           w_aaa2, w_gate2, rw_k_k, rw_k_a, rw_r_k, rw_gn_g, rw_gn_b, w_mem_kv, w_sb_o, w_mem_o,
           w_rw_o, w_out, ln1_g, ln1_b, w_router, b_router, w_gate_up, b_gate_up, w_down, b_down,
           ln2_g, ln2_b):
    assert w_in.shape[0] == DEPTH == 1
    l = 0
    lw = dict(mu_rw=mu_rw[l], w_decay0=w_decay0[l], w_decay2=w_decay2[l], w_aaa0=w_aaa0[l],
              w_aaa2=w_aaa2[l], w_gate2=w_gate2[l], rw_k_k=rw_k_k[l], rw_k_a=rw_k_a[l],
              rw_r_k=rw_r_k[l], rw_gn_g=rw_gn_g[l], rw_gn_b=rw_gn_b[l], w_sb_o=w_sb_o[l],
              w_mem_o=w_mem_o[l], w_rw_o=w_rw_o[l], w_out=w_out[l], ln1_g=ln1_g[l], ln1_b=ln1_b[l])
    bp, sp, _ = x_prompt.shape
    bs, ss, _ = x_sample.shape
    w_in_b = w_in[l].astype(bf16)

    xp = x_prompt.reshape(bp * sp, D_MODEL)
    q_sb, k_sb, v_sb, *rest = _proj(xp, w_in_b, IN_SPLITS, PROJ_TILE_M)
    tok3 = lambda a: a.reshape(bp, sp, a.shape[-1])
    q_sb, k_sb, v_sb = tok3(q_sb), tok3(k_sb), tok3(v_sb)
    o_sb = _sb_prompt(q_sb, k_sb, v_sb, sb_bias[l])
    mem_k, mem_v = _split(mem_prompt @ w_mem_kv[l], (MEM_WIDTH, MEM_WIDTH))
    mem_k = mem_k.reshape(bp, N_MEM, MEM_HEADS, MEM_HEAD_DIM)
    mem_v = mem_v.reshape(bp, N_MEM, MEM_HEADS, MEM_HEAD_DIM)
    h_p, shift_p, wkv_p = _mix(
        x_prompt, [tok3(a) for a in rest], o_sb, mem_k, mem_v, jnp.zeros((bp, RW_SHIFT_WIDTH), f32),
        jnp.zeros((bp, RW_HEADS, RW_HEAD_DIM, RW_HEAD_DIM), f32), lw)

    xs = x_sample.reshape(bs * ss, D_MODEL)
    q_s, k_s, v_s, *rest_s = _proj(xs, w_in_b, IN_SPLITS, bs * ss)
    tok3s = lambda a: a.reshape(bs, ss, a.shape[-1])
    heads_s = lambda a: a.reshape(bs, ss, SB_HEADS, SB_HEAD_DIM)
    o_sb_s = _sb_sample(heads_s(q_s), heads_s(k_s), heads_s(v_s), sb_bias[l], cache_sb_k[l], cache_sb_v[l],
                        page_table)
    h_s, shift_s, wkv_s = _mix(
        x_sample, [tok3s(a) for a in rest_s], o_sb_s, cache_mem_k[l], cache_mem_v[l], state_rw_shift[l],
        state_rw_wkv[l], lw)

    h_all = jnp.concatenate([h_p.reshape(bp * sp, D_MODEL), h_s.reshape(bs * ss, D_MODEL)], axis=0)
    moe = _moe(h_all, w_router[l], b_router[l], w_gate_up[l], b_gate_up[l], w_down[l], b_down[l])
    y_all = _layer_norm(DN_ALPHA * h_all + moe, ln2_g[l], ln2_b[l])
    y_prompt = y_all[:bp * sp].reshape(bp, sp, D_MODEL)
    y_sample = y_all[bp * sp:].reshape(bs, ss, D_MODEL)

    return (y_prompt, y_sample,
            k_sb.reshape(1, bp, sp, SB_HEADS, SB_HEAD_DIM), v_sb.reshape(1, bp, sp, SB_HEADS, SB_HEAD_DIM),
            mem_k[None], mem_v[None], shift_p[None], wkv_p[None],
            k_s.reshape(1, bs, ss, SB_HEADS, SB_HEAD_DIM), v_s.reshape(1, bs, ss, SB_HEADS, SB_HEAD_DIM),
            shift_s[None], wkv_s[None])
```

```python
import functools

import jax
import jax.numpy as jnp
import numpy as np
from jax import lax
from jax.experimental import pallas as pl
from jax.experimental.pallas import tpu as pltpu

D_MODEL = 1024
PAGE_SIZE = 128
N_MEM = 256
SB_HEADS = 8
SB_HEAD_DIM = 64
SB_WIDTH = SB_HEADS * SB_HEAD_DIM
SB_SCALE = SB_HEAD_DIM ** -0.5
MEM_HEADS = 4
MEM_HEAD_DIM = 128
MEM_WIDTH = MEM_HEADS * MEM_HEAD_DIM
MEM_SCALE = MEM_HEAD_DIM ** -0.5
RW_HEADS = 8
RW_HEAD_DIM = 64
RW_WIDTH = RW_HEADS * RW_HEAD_DIM
DECAY_LORA = 64
AAA_LORA = 64
GATE_LORA = 128
RW_SPLITS = (RW_WIDTH, RW_WIDTH, RW_WIDTH, DECAY_LORA, AAA_LORA, GATE_LORA)
RW_SHIFT_WIDTH = sum(RW_SPLITS)
RW_GN_EPS = 64e-5
N_EXPERTS = 32
TOP_K = 4
D_FF = D_MODEL
SWIGLU_ALPHA = 1.702
SWIGLU_LIMIT = 7.0
DEPTH = 1
DN_ALPHA = (2 * DEPTH) ** 0.25
LN_EPS = 1e-5
IN_SPLITS = (SB_WIDTH, SB_WIDTH, SB_WIDTH, MEM_WIDTH, RW_SHIFT_WIDTH, D_MODEL, D_MODEL, D_MODEL)
IN_WIDTH = sum(IN_SPLITS)

LANES = 128
MXU_DIM = 256
VMEM_LIMIT_BYTES = 48 * 1024 * 1024

SB_TILE = MXU_DIM
PROJ_TILE_M = 256
MOE_TILE_M = 256

f32 = jnp.float32
bf16 = jnp.bfloat16


def _split(x, sizes):
    return jnp.split(x, np.cumsum(sizes)[:-1].tolist(), axis=-1)


def _proj_kernel(x_ref, w_ref, *o_refs, splits):
    xb = x_ref[...].astype(bf16)
    off = 0
    for o_ref, n in zip(o_refs, splits):
        o_ref[...] = jnp.dot(xb, w_ref[:, off:off + n], preferred_element_type=f32)
        off += n


def _proj(x, w_bf16, splits, tile_m):
    t, d = x.shape
    n = w_bf16.shape[1]
    assert t % tile_m == 0 and sum(splits) == n
    return pl.pallas_call(
        functools.partial(_proj_kernel, splits=splits),
        grid=(t // tile_m,),
        in_specs=[
            pl.BlockSpec((tile_m, d), lambda i: (i, 0)),
            pl.BlockSpec((d, n), lambda i: (0, 0), pipeline_mode=pl.Buffered(1)),
        ],
        out_specs=[pl.BlockSpec((tile_m, s), lambda i: (i, 0)) for s in splits],
        out_shape=[jax.ShapeDtypeStruct((t, s), f32) for s in splits],
        compiler_params=pltpu.CompilerParams(
            dimension_semantics=("arbitrary",), vmem_limit_bytes=VMEM_LIMIT_BYTES),
        name="proj_in",
    )(x, w_bf16)


def _sb_prompt_kernel(bias_ref, q_ref, k_ref, v_ref, o_ref, *, tile):
    hp = pl.program_id(1)
    qi = pl.program_id(2)
    row = lax.broadcasted_iota(jnp.int32, (tile, tile), 0)
    col = lax.broadcasted_iota(jnp.int32, (tile, tile), 1)
    later = (row > col).astype(bf16)
    causal = col < row
    q_all = q_ref[...] * SB_SCALE
    outs = []
    for hh in range(LANES // SB_HEAD_DIM):
        lo, hi = hh * SB_HEAD_DIM, (hh + 1) * SB_HEAD_DIM
        qh = q_all[:, lo:hi].astype(bf16)
        bias = bias_ref[hp * (LANES // SB_HEAD_DIM) + hh]

        def key_tile(j, carry, acc, masked, qh=qh, bias=bias, lo=lo, hi=hi):
            start = pl.multiple_of(j * tile, tile)
            kh = k_ref[pl.ds(start, tile), lo:hi].astype(bf16)
            vh = v_ref[pl.ds(start, tile), lo:hi].astype(bf16)
            z = lax.dot_general(qh, kh, (((1,), (1,)), ((), ())), preferred_element_type=f32) + bias
            softplus = jnp.maximum(z, 0.0) + jnp.log(1.0 + jnp.exp(-jnp.abs(z)))
            log_keep = -softplus
            if masked:
                log_keep = jnp.where(causal, log_keep, 0.0)
            inner = jnp.dot(log_keep.astype(bf16), later, preferred_element_type=f32)
            a = jnp.exp(z - softplus + carry + inner)
            if masked:
                a = jnp.where(causal, a, 0.0)
            acc = acc + jnp.dot(a.astype(bf16), vh, preferred_element_type=f32)
            carry = carry + jnp.sum(log_keep, axis=1, keepdims=True)
            return carry, acc

        carry = jnp.zeros((tile, 1), f32)
        acc = jnp.zeros((tile, SB_HEAD_DIM), f32)
        carry, acc = key_tile(qi, carry, acc, True)
        carry, acc = lax.fori_loop(
            0, qi, lambda s, c, key_tile=key_tile: key_tile(qi - 1 - s, c[0], c[1], False), (carry, acc))
        outs.append(acc)
    o_ref[...] = jnp.concatenate(outs, axis=1)


def _sb_prompt(q, k, v, bias, tile=SB_TILE):
    b, s, w = q.shape
    assert s % tile == 0 and w % LANES == 0
    blk = lambda bb, hp, qi: (bb, qi, hp)
    full = lambda bb, hp, qi: (bb, 0, hp)
    return pl.pallas_call(
        functools.partial(_sb_prompt_kernel, tile=tile),
        grid=(b, w // LANES, s // tile),
        in_specs=[
            pl.BlockSpec(memory_space=pltpu.SMEM),
            pl.BlockSpec((None, tile, LANES), blk),
            pl.BlockSpec((None, s, LANES), full),
            pl.BlockSpec((None, s, LANES), full),
        ],
        out_specs=pl.BlockSpec((None, tile, LANES), blk),
        out_shape=jax.ShapeDtypeStruct((b, s, w), f32),
        compiler_params=pltpu.CompilerParams(
            dimension_semantics=("arbitrary", "arbitrary", "arbitrary"),
            vmem_limit_bytes=VMEM_LIMIT_BYTES),
        name="sb_prompt",
    )(bias, q, k, v)


SBS_PAGES_PER_STEP = 8


def _sb_sample_kernel(pt_ref, bias_ref, q_ref, *refs, n_slots):
    del pt_ref
    k_refs = refs[:n_slots]
    v_refs = refs[n_slots:2 * n_slots]
    o_ref = refs[2 * n_slots]
    qb_ref, acc_ref, carry_ref, z_ref = refs[2 * n_slots + 1:]
    step = pl.program_id(1)

    @pl.when(step == 0)
    def _():
        q = q_ref[...] * SB_SCALE
        qb_ref[...] = jnp.broadcast_to(q, qb_ref.shape)
        acc_ref[...] = jnp.zeros_like(acc_ref)
        carry_ref[...] = jnp.zeros_like(carry_ref)

    row = lax.broadcasted_iota(jnp.int32, (PAGE_SIZE, PAGE_SIZE), 0)
    col = lax.broadcasted_iota(jnp.int32, (PAGE_SIZE, PAGE_SIZE), 1)
    later = (row > col).astype(bf16)
    later2 = jnp.concatenate([later, later], axis=0)

    for i, k_ref in enumerate(k_refs):
        for h in range(SB_HEADS):
            zh = jnp.sum(k_ref[h] * qb_ref[h], axis=0, keepdims=True) + bias_ref[h]
            z_ref[pl.ds(i * SB_HEADS + h, 1), :] = zh
    z = z_ref[...]
    softplus = jnp.maximum(z, 0.0) + jnp.log(1.0 + jnp.exp(-jnp.abs(z)))
    log_keep = -softplus
    hi = log_keep.astype(bf16)
    lo = (log_keep - hi.astype(f32)).astype(bf16)
    inner = jnp.dot(jnp.concatenate([hi, lo], axis=1), later2, preferred_element_type=f32)
    total = jnp.sum(log_keep, axis=1, keepdims=True)
    base = z - softplus + inner
    carry = carry_ref[...]
    for i in range(n_slots):
        sl = slice(i * SB_HEADS, (i + 1) * SB_HEADS)
        w = jnp.exp(base[sl] + carry)
        for h in range(SB_HEADS):
            acc_ref[h] += v_refs[i][h] * w[h:h + 1, :]
        carry = carry + total[sl]
    carry_ref[...] = carry

    @pl.when(step == pl.num_programs(1) - 1)
    def _():
        o_ref[...] = jnp.sum(acc_ref[...], axis=2, keepdims=True)


def _sb_sample_past(q, bias, k_pool, v_pool, page_table):
    b, n_pages = page_table.shape
    n_slots = SBS_PAGES_PER_STEP
    assert n_pages % n_slots == 0
    page_block = (None, SB_HEADS, SB_HEAD_DIM, PAGE_SIZE)

    def page_spec(slot):
        return pl.BlockSpec(
            page_block, lambda bb, s, pt, slot=slot: (pt[bb, n_pages - 1 - (s * n_slots + slot)], 0, 0, 0))

    grid_spec = pltpu.PrefetchScalarGridSpec(
        num_scalar_prefetch=1,
        grid=(b, n_pages // n_slots),
        in_specs=[
            pl.BlockSpec(memory_space=pltpu.SMEM),
            pl.BlockSpec((None, SB_HEADS, SB_HEAD_DIM, 1), lambda bb, s, pt: (bb, 0, 0, 0)),
        ] + [page_spec(i) for i in range(n_slots)] * 2,
        out_specs=pl.BlockSpec((None, SB_HEADS, SB_HEAD_DIM, 1), lambda bb, s, pt: (bb, 0, 0, 0)),
        scratch_shapes=[
            pltpu.VMEM((SB_HEADS, SB_HEAD_DIM, PAGE_SIZE), f32),
            pltpu.VMEM((SB_HEADS, SB_HEAD_DIM, PAGE_SIZE), f32),
            pltpu.VMEM((SB_HEADS, 1), f32),
            pltpu.VMEM((n_slots * SB_HEADS, PAGE_SIZE), f32),
        ],
    )
    o = pl.pallas_call(
        functools.partial(_sb_sample_kernel, n_slots=n_slots),
        grid_spec=grid_spec,
        out_shape=jax.ShapeDtypeStruct((b, SB_HEADS, SB_HEAD_DIM, 1), f32),
        compiler_params=pltpu.CompilerParams(
            dimension_semantics=("arbitrary", "arbitrary"), vmem_limit_bytes=VMEM_LIMIT_BYTES),
        name="sb_sample",
    )(page_table, bias, q[..., None], *([k_pool] * n_slots), *([v_pool] * n_slots))
    return o[..., 0]


def _moe_kernel(be_ref, used_ref, x_ref, wgu_ref, bg_ref, bl_ref, wd_ref, bd_ref, rw_ref, o_ref,
                wg_s, wl_s, wd_s):
    i = pl.program_id(0)
    ff = wd_ref.shape[0]
    half = MXU_DIM // 2

    src = lax.broadcasted_iota(jnp.int32, (MXU_DIM, MXU_DIM), 0)
    dst = lax.broadcasted_iota(jnp.int32, (MXU_DIM, MXU_DIM), 1)
    unzip = (src == jnp.where(dst < half, 2 * dst, 2 * (dst - half) + 1)).astype(bf16)

    @pl.when((i == 0) | (be_ref[i] != be_ref[jnp.maximum(i - 1, 0)]))
    def _():
        for c in range(2 * ff // MXU_DIM):
            cols = slice(c * MXU_DIM, (c + 1) * MXU_DIM)
            out = slice(c * half, (c + 1) * half)
            w = jnp.dot(wgu_ref[:, cols].astype(bf16), unzip, preferred_element_type=f32).astype(bf16)
            wg_s[:, out] = w[:, :half]
            wl_s[:, out] = w[:, half:]
        wd_s[...] = wd_ref[...].astype(bf16)

    @pl.when(i < used_ref[0])
    def _():
        x = x_ref[...]
        g = jnp.dot(x, wg_s[...], preferred_element_type=f32) + bg_ref[...]
        l = jnp.dot(x, wl_s[...], preferred_element_type=f32) + bl_ref[...]
        glu = jnp.minimum(g, SWIGLU_LIMIT)
        lin = jnp.clip(l, -SWIGLU_LIMIT, SWIGLU_LIMIT)
        h = glu * (1.0 / (1.0 + jnp.exp(-SWIGLU_ALPHA * glu))) * (lin + 1.0)
        y = jnp.dot(h.astype(bf16), wd_s[...], preferred_element_type=f32) + bd_ref[...]
        o_ref[...] = y * rw_ref[...]

    @pl.when(i >= used_ref[0])
    def _():
        o_ref[...] = jnp.zeros_like(o_ref)


def _moe_experts(xb, block_e, n_used, w_gate_up, b_gate_up, w_down, b_down, row_w, tile_m):
    n_rows, d = xb.shape
    n_blocks = n_rows // tile_m
    ff = w_down.shape[1]
    row = lambda i, be, used: (i, 0)
    wsel = lambda i, be, used: (be[i], 0, 0)
    grid_spec = pltpu.PrefetchScalarGridSpec(
        num_scalar_prefetch=2,
        grid=(n_blocks,),
        in_specs=[
            pl.BlockSpec((tile_m, d), row),
            pl.BlockSpec((None, d, 2 * ff), wsel),
            pl.BlockSpec((None, 1, ff), wsel),
            pl.BlockSpec((None, 1, ff), wsel),
            pl.BlockSpec((None, ff, d), wsel),
            pl.BlockSpec((None, 1, d), wsel),
            pl.BlockSpec((tile_m, 1), row),
        ],
        out_specs=pl.BlockSpec((tile_m, d), row),
        scratch_shapes=[
            pltpu.VMEM((d, ff), bf16), pltpu.VMEM((d, ff), bf16), pltpu.VMEM((ff, d), bf16),
        ],
    )
    return pl.pallas_call(
        _moe_kernel,
        grid_spec=grid_spec,
        out_shape=jax.ShapeDtypeStruct((n_rows, d), f32),
        compiler_params=pltpu.CompilerParams(
            dimension_semantics=("arbitrary",), vmem_limit_bytes=VMEM_LIMIT_BYTES),
        name="moe_experts",
    )(block_e, n_used, xb, w_gate_up, b_gate_up[:, None, 0::2], b_gate_up[:, None, 1::2], w_down,
      b_down[:, None, :], row_w)


def _moe(x, w_router, b_router, w_gate_up, b_gate_up, w_down, b_down, tile_m=MOE_TILE_M):
    t = x.shape[0]
    logits = jnp.dot(x, w_router, preferred_element_type=f32, precision=lax.Precision.HIGHEST) + b_router
    top_logit, top_e = lax.top_k(logits, TOP_K)
    top_p = jax.nn.softmax(top_logit, axis=-1)
    n_pairs = t * TOP_K
    n_blocks = -(-n_pairs // tile_m) + N_EXPERTS
    pair_e = top_e.reshape(-1)
    onehot = (pair_e[:, None] == jnp.arange(N_EXPERTS, dtype=pair_e.dtype)[None, :]).astype(jnp.int32)
    counts = onehot.sum(0)
    rank = jnp.take_along_axis(jnp.cumsum(onehot, axis=0) - onehot, pair_e[:, None], axis=1)[:, 0]
    padded = (counts + tile_m - 1) // tile_m * tile_m
    pad_end = jnp.cumsum(padded)
    dest = (pad_end - padded)[pair_e] + rank
    n_rows = n_blocks * tile_m
    row_tok = jnp.zeros((n_rows,), jnp.int32).at[dest].set(jnp.arange(n_pairs, dtype=jnp.int32) // TOP_K)
    row_w = jnp.zeros((n_rows,), f32).at[dest].set(top_p.reshape(-1))
    block_e = jnp.minimum(
        jnp.searchsorted(pad_end, jnp.arange(n_blocks, dtype=jnp.int32) * tile_m, side='right'),
        N_EXPERTS - 1).astype(jnp.int32)
    n_used = (pad_end[-1:] // tile_m).astype(jnp.int32)
    xb = x.astype(bf16)[row_tok]
    yb = _moe_experts(xb, block_e, n_used, w_gate_up, b_gate_up, w_down, b_down, row_w[:, None], tile_m)
    return yb[dest.reshape(t, TOP_K)].sum(axis=1)


RW_LANE_HALF = LANES // 2
RW_KJ = RW_HEAD_DIM // 2
RW_VG = RW_HEAD_DIM // 8
RW_STEPS_PER_BLOCK = 32


def _rwkv_scan_kernel(kk_ref, d_ref, kka_ref, km_ref, r_ref, v_ref, s0_ref, y_ref, s_ref, *, steps):
    tb = pl.program_id(1)

    @pl.when(tb == 0)
    def _():
        s_ref[...] = s0_ref[...]

    def fold(x):
        return x + pltpu.roll(x, RW_LANE_HALF, axis=1)

    def step(t, carry):
        acc = [None] * RW_VG
        for j in range(RW_KJ):
            kkj = kk_ref[t, pl.ds(j, 1), :]
            for g in range(RW_VG):
                p = s_ref[g, j] * kkj
                acc[g] = p if acc[g] is None else acc[g] + p
        sa = [-fold(a) for a in acc]
        vv = [v_ref[t, g] for g in range(RW_VG)]
        yacc = [None] * RW_VG
        for j in range(RW_KJ):
            dj = d_ref[t, pl.ds(j, 1), :]
            kkaj = kka_ref[t, pl.ds(j, 1), :]
            kmj = km_ref[t, pl.ds(j, 1), :]
            rj = r_ref[t, pl.ds(j, 1), :]
            for g in range(RW_VG):
                s = s_ref[g, j] * dj + sa[g] * kkaj + vv[g] * kmj
                s_ref[g, j] = s
                p = s * rj
                yacc[g] = p if yacc[g] is None else yacc[g] + p
        for g in range(RW_VG):
            y_ref[t, g] = fold(yacc[g])
        return carry

    lax.fori_loop(0, steps, step, 0)


def _rwkv_scan(kk, d, kka, km, r, v, s0):
    n_g, s_len = kk.shape[:2]
    steps = min(RW_STEPS_PER_BLOCK, s_len)
    assert s_len % steps == 0
    krow = pl.BlockSpec((None, steps, RW_KJ, LANES), lambda g, t: (g, t, 0, 0))
    vrow = pl.BlockSpec((None, steps, RW_VG, 8, LANES), lambda g, t: (g, t, 0, 0, 0))
    st = pl.BlockSpec((None, RW_VG, RW_KJ, 8, LANES), lambda g, t: (g, 0, 0, 0, 0))
    return pl.pallas_call(
        functools.partial(_rwkv_scan_kernel, steps=steps),
        grid=(n_g, s_len // steps),
        in_specs=[krow, krow, krow, krow, krow, vrow, st],
        out_specs=[vrow, st],
        out_shape=[jax.ShapeDtypeStruct(v.shape, f32), jax.ShapeDtypeStruct(s0.shape, f32)],
        compiler_params=pltpu.CompilerParams(
            dimension_semantics=("arbitrary", "arbitrary"), vmem_limit_bytes=VMEM_LIMIT_BYTES),
        name="rwkv_scan",
    )(kk, d, kka, km, r, v, s0)


def _rw_groups(n_bh):
    assert n_bh % RW_LANE_HALF == 0
    return n_bh // RW_LANE_HALF


def _rw_k_layout(x):
    b, s, h, _ = x.shape
    g = _rw_groups(b * h)
    x = x.reshape(b, s, h, 2, RW_KJ)
    x = jnp.transpose(x, (1, 4, 3, 0, 2)).reshape(s, RW_KJ, 2, g, RW_LANE_HALF)
    return jnp.transpose(x, (3, 0, 1, 2, 4)).reshape(g, s, RW_KJ, LANES)


def _rw_v_layout(x):
    b, s, h, _ = x.shape
    g = _rw_groups(b * h)
    x = jnp.transpose(x, (1, 3, 0, 2)).reshape(s, RW_HEAD_DIM, g, RW_LANE_HALF)
    x = jnp.transpose(x, (2, 0, 1, 3))
    x = jnp.concatenate([x, x], axis=-1)
    return x.reshape(g, s, RW_VG, 8, LANES)


def _rw_v_unlayout(y, b, h):
    g, s = y.shape[:2]
    y = y.reshape(g, s, RW_HEAD_DIM, LANES)[..., :RW_LANE_HALF]
    y = jnp.transpose(y, (1, 2, 0, 3)).reshape(s, RW_HEAD_DIM, b, h)
    return jnp.transpose(y, (2, 0, 3, 1))


def _rw_state_layout(wkv):
    b, h = wkv.shape[:2]
    g = _rw_groups(b * h)
    x = wkv.reshape(g, RW_LANE_HALF, RW_VG, 8, 2, RW_KJ)
    return jnp.transpose(x, (0, 2, 5, 3, 4, 1)).reshape(g, RW_VG, RW_KJ, 8, LANES)


def _rw_state_unlayout(st, b, h):
    g = st.shape[0]
    x = st.reshape(g, RW_VG, RW_KJ, 8, 2, RW_LANE_HALF)
    x = jnp.transpose(x, (0, 5, 1, 3, 4, 2))
    return x.reshape(b, h, RW_HEAD_DIM, RW_HEAD_DIM)


def _layer_norm(x, g, b):
    mu = x.mean(-1, keepdims=True)
    var = jnp.square(x - mu).mean(-1, keepdims=True)
    return (x - mu) * lax.rsqrt(var + LN_EPS) * g + b


def _sb_logits(q, k, bias):
    z = jnp.einsum('bqhd,bkhd->bhqk', q, k, preferred_element_type=f32) * SB_SCALE
    return z + bias.astype(f32)[None, :, None, None]


def _sb_weights(z, causal):
    log_keep = jnp.where(causal, jax.nn.log_sigmoid(-z), 0.0)
    log_survive = lax.cumsum(log_keep, axis=z.ndim - 1, reverse=True) - log_keep
    return jnp.where(causal, jnp.exp(jax.nn.log_sigmoid(z) + log_survive), 0.0)


def _sb_sample(q, k_new, v_new, bias, k_pool, v_pool, page_table):
    b, s = q.shape[:2]
    assert s == 1
    to_token_minor = lambda pool: jnp.transpose(pool, (0, 2, 3, 1))
    o_past = _sb_sample_past(q[:, 0], bias, to_token_minor(k_pool), to_token_minor(v_pool), page_table)
    new_pos = jnp.arange(s)
    w_new = _sb_weights(_sb_logits(q, k_new, bias), new_pos[None, :] < new_pos[:, None])
    o_new = jnp.einsum('bhqk,bkhd->bqhd', w_new, v_new)
    return (o_past[:, None] + o_new).reshape(b, s, SB_WIDTH)


def _mem_attend(q, mem_k, mem_v):
    s = jnp.einsum('bshd,bmhd->bhsm', q, mem_k, preferred_element_type=f32) * MEM_SCALE
    p = jax.nn.softmax(s, axis=-1)
    return jnp.einsum('bhsm,bmhd->bshd', p, mem_v)


def _rwkv7(p_rw, shift0, wkv0, mu_rw, w_decay0, w_decay2, w_aaa0, w_aaa2, w_gate2,
           rw_k_k, rw_k_a, rw_r_k, rw_gn_g, rw_gn_b):
    b, s, _ = p_rw.shape
    prev = jnp.concatenate([shift0[:, None, :], p_rw[:, :-1]], axis=1)
    xs = p_rw + mu_rw * (prev - p_rw)
    r, k, v, w_lo, a_lo, g_lo = _split(xs, RW_SPLITS)
    w_raw = w_decay0 + jnp.tanh(w_lo) @ w_decay2
    decay = jnp.exp(-jnp.exp(-jax.nn.softplus(-w_raw) - 0.5))
    a = jax.nn.sigmoid(w_aaa0 + a_lo @ w_aaa2)
    g = jax.nn.sigmoid(g_lo) @ w_gate2
    heads = lambda t: t.reshape(b, s, RW_HEADS, RW_HEAD_DIM)
    kk = heads(k * rw_k_k)
    kk = kk / jnp.maximum(jnp.sqrt(jnp.sum(kk * kk, axis=-1, keepdims=True)), 1e-12)
    k_mod = k * (1.0 + (a - 1.0) * rw_k_a)
    rh, kh, vh, ah, dh = heads(r), heads(k_mod), heads(v), heads(a), heads(decay)

    y, wkv = _rwkv_scan(_rw_k_layout(kk), _rw_k_layout(dh), _rw_k_layout(kk * ah), _rw_k_layout(kh),
                        _rw_k_layout(rh), _rw_v_layout(vh), _rw_state_layout(wkv0))
    y = _rw_v_unlayout(y, b, RW_HEADS)
    wkv = _rw_state_unlayout(wkv, b, RW_HEADS)
    mu = y.mean(-1, keepdims=True)
    var = jnp.square(y - mu).mean(-1, keepdims=True)
    y = ((y - mu) * lax.rsqrt(var + RW_GN_EPS)).reshape(b, s, RW_WIDTH) * rw_gn_g + rw_gn_b
    bonus = jnp.sum(rh * kh * rw_r_k, axis=-1, keepdims=True) * vh
    y = (y + bonus.reshape(b, s, RW_WIDTH)) * g
    return y, p_rw[:, -1], wkv


def _mix(x, parts, o_sb, mem_k, mem_v, shift0, wkv0, lw):
    b, s, _ = x.shape
    q_mem, p_rw, g_sb, g_mem, g_rw = parts
    o_mem = _mem_attend(q_mem.reshape(b, s, MEM_HEADS, MEM_HEAD_DIM), mem_k, mem_v).reshape(b, s, MEM_WIDTH)
    o_rw, shift, wkv = _rwkv7(p_rw, shift0, wkv0, lw['mu_rw'], lw['w_decay0'], lw['w_decay2'],
                              lw['w_aaa0'], lw['w_aaa2'], lw['w_gate2'], lw['rw_k_k'], lw['rw_k_a'],
                              lw['rw_r_k'], lw['rw_gn_g'], lw['rw_gn_b'])
    merged = (jax.nn.sigmoid(g_sb) * (o_sb @ lw['w_sb_o']) + jax.nn.sigmoid(g_mem) * (o_mem @ lw['w_mem_o'])
              + jax.nn.sigmoid(g_rw) * (o_rw @ lw['w_rw_o']))
    h = _layer_norm(DN_ALPHA * x + merged @ lw['w_out'], lw['ln1_g'], lw['ln1_b'])
    return h, shift, wkv


def kernel(x_prompt, x_sample, cache_sb_k, cache_sb_v, cache_mem_k, cache_mem_v, state_rw_shift,
           state_rw_wkv, page_table, mem_prompt, w_in, sb_bias, mu_rw, w_decay0, w_decay2, w_aaa0,
           w_aaa2, w_gate2, rw_k_k, rw_k_a, rw_r_k, rw_gn_g, rw_gn_b, w_mem_kv, w_sb_o, w_mem_o,
           w_rw_o, w_out, ln1_g, ln1_b, w_router, b_router, w_gate_up, b_gate_up, w_down, b_down,
           ln2_g, ln2_b):
    assert w_in.shape[0] == DEPTH == 1
    l = 0
    lw = dict(mu_rw=mu_rw[l], w_decay0=w_decay0[l], w_decay2=w_decay2[l], w_aaa0=w_aaa0[l],
              w_aaa2=w_aaa2[l], w_gate2=w_gate2[l], rw_k_k=rw_k_k[l], rw_k_a=rw_k_a[l],
              rw_r_k=rw_r_k[l], rw_gn_g=rw_gn_g[l], rw_gn_b=rw_gn_b[l], w_sb_o=w_sb_o[l],
              w_mem_o=w_mem_o[l], w_rw_o=w_rw_o[l], w_out=w_out[l], ln1_g=ln1_g[l], ln1_b=ln1_b[l])
    bp, sp, _ = x_prompt.shape
    bs, ss, _ = x_sample.shape
    w_in_b = w_in[l].astype(bf16)

    xp = x_prompt.reshape(bp * sp, D_MODEL)
    q_sb, k_sb, v_sb, *rest = _proj(xp, w_in_b, IN_SPLITS, PROJ_TILE_M)
    tok3 = lambda a: a.reshape(bp, sp, a.shape[-1])
    q_sb, k_sb, v_sb = tok3(q_sb), tok3(k_sb), tok3(v_sb)
    o_sb = _sb_prompt(q_sb, k_sb, v_sb, sb_bias[l])
    mem_k, mem_v = _split(mem_prompt @ w_mem_kv[l], (MEM_WIDTH, MEM_WIDTH))
    mem_k = mem_k.reshape(bp, N_MEM, MEM_HEADS, MEM_HEAD_DIM)
    mem_v = mem_v.reshape(bp, N_MEM, MEM_HEADS, MEM_HEAD_DIM)
    h_p, shift_p, wkv_p = _mix(
        x_prompt, [tok3(a) for a in rest], o_sb, mem_k, mem_v, jnp.zeros((bp, RW_SHIFT_WIDTH), f32),
        jnp.zeros((bp, RW_HEADS, RW_HEAD_DIM, RW_HEAD_DIM), f32), lw)

    xs = x_sample.reshape(bs * ss, D_MODEL)
    q_s, k_s, v_s, *rest_s = _proj(xs, w_in_b, IN_SPLITS, bs * ss)
    tok3s = lambda a: a.reshape(bs, ss, a.shape[-1])
    heads_s = lambda a: a.reshape(bs, ss, SB_HEADS, SB_HEAD_DIM)
    o_sb_s = _sb_sample(heads_s(q_s), heads_s(k_s), heads_s(v_s), sb_bias[l], cache_sb_k[l], cache_sb_v[l],
                        page_table)
    h_s, shift_s, wkv_s = _mix(
        x_sample, [tok3s(a) for a in rest_s], o_sb_s, cache_mem_k[l], cache_mem_v[l], state_rw_shift[l],
        state_rw_wkv[l], lw)

    h_all = jnp.concatenate([h_p.reshape(bp * sp, D_MODEL), h_s.reshape(bs * ss, D_MODEL)], axis=0)
    moe = _moe(h_all, w_router[l], b_router[l], w_gate_up[l], b_gate_up[l], w_down[l], b_down[l])
    y_all = _layer_norm(DN_ALPHA * h_all + moe, ln2_g[l], ln2_b[l])
    y_prompt = y_all[:bp * sp].reshape(bp, sp, D_MODEL)
    y_sample = y_all[bp * sp:].reshape(bs, ss, D_MODEL)

    return (y_prompt, y_sample,
            k_sb.reshape(1, bp, sp, SB_HEADS, SB_HEAD_DIM), v_sb.reshape(1, bp, sp, SB_HEADS, SB_HEAD_DIM),
            mem_k[None], mem_v[None], shift_p[None], wkv_p[None],
            k_s.reshape(1, bs, ss, SB_HEADS, SB_HEAD_DIM), v_s.reshape(1, bs, ss, SB_HEADS, SB_HEAD_DIM),
            shift_s[None], wkv_s[None])
```

```python
import functools

import jax
import jax.numpy as jnp
import numpy as np
from jax import lax
from jax.experimental import pallas as pl
from jax.experimental.pallas import tpu as pltpu

D_MODEL = 1024
PAGE_SIZE = 128
N_MEM = 256
SB_HEADS = 8
SB_HEAD_DIM = 64
SB_WIDTH = SB_HEADS * SB_HEAD_DIM
SB_SCALE = SB_HEAD_DIM ** -0.5
LOG2_E = 1.4426950408889634
MEM_HEADS = 4
MEM_HEAD_DIM = 128
MEM_WIDTH = MEM_HEADS * MEM_HEAD_DIM
MEM_SCALE = MEM_HEAD_DIM ** -0.5
RW_HEADS = 8
RW_HEAD_DIM = 64
RW_WIDTH = RW_HEADS * RW_HEAD_DIM
DECAY_LORA = 64
AAA_LORA = 64
GATE_LORA = 128
RW_SPLITS = (RW_WIDTH, RW_WIDTH, RW_WIDTH, DECAY_LORA, AAA_LORA, GATE_LORA)
RW_SHIFT_WIDTH = sum(RW_SPLITS)
RW_GN_EPS = 64e-5
N_EXPERTS = 32
TOP_K = 4
D_FF = D_MODEL
SWIGLU_ALPHA = 1.702
SWIGLU_LIMIT = 7.0
DEPTH = 1
DN_ALPHA = (2 * DEPTH) ** 0.25
LN_EPS = 1e-5
IN_SPLITS = (SB_WIDTH, SB_WIDTH, SB_WIDTH, MEM_WIDTH, RW_SHIFT_WIDTH, D_MODEL, D_MODEL, D_MODEL)
IN_WIDTH = sum(IN_SPLITS)

LANES = 128
MXU_DIM = 256
VMEM_LIMIT_BYTES = 48 * 1024 * 1024

SB_TILE = MXU_DIM
SB_STEP_WIDTH = 4 * SB_HEAD_DIM
PROJ_TILE_M = 256
MOE_TILE_M = 256

f32 = jnp.float32
bf16 = jnp.bfloat16


def _split(x, sizes):
    return jnp.split(x, np.cumsum(sizes)[:-1].tolist(), axis=-1)


def _proj_kernel(x_ref, w_ref, *o_refs, splits):
    xb = x_ref[...].astype(bf16)
    off = 0
    for o_ref, n in zip(o_refs, splits):
        o_ref[...] = jnp.dot(xb, w_ref[:, off:off + n], preferred_element_type=f32)
        off += n


def _proj(x, w_bf16, splits, tile_m):
    t, d = x.shape
    n = w_bf16.shape[1]
    assert t % tile_m == 0 and sum(splits) == n
    return pl.pallas_call(
        functools.partial(_proj_kernel, splits=splits),
        grid=(t // tile_m,),
        in_specs=[
            pl.BlockSpec((tile_m, d), lambda i: (i, 0)),
            pl.BlockSpec((d, n), lambda i: (0, 0), pipeline_mode=pl.Buffered(1)),
        ],
        out_specs=[pl.BlockSpec((tile_m, s), lambda i: (i, 0)) for s in splits],
        out_shape=[jax.ShapeDtypeStruct((t, s), f32) for s in splits],
        compiler_params=pltpu.CompilerParams(
            dimension_semantics=("arbitrary",), vmem_limit_bytes=VMEM_LIMIT_BYTES),
        name="proj_in",
    )(x, w_bf16)


def _sb_prompt_kernel(bias_ref, q_ref, k_ref, v_ref, o_ref, *, tile):
    hp = pl.program_id(1)
    qi = pl.program_id(2)
    heads = q_ref.shape[-1] // SB_HEAD_DIM
    row = lax.broadcasted_iota(jnp.int32, (tile, tile), 0)
    col = lax.broadcasted_iota(jnp.int32, (tile, tile), 1)
    later = (row > col).astype(bf16)
    causal = col < row
    q_all = q_ref[...] * (SB_SCALE * LOG2_E)
    qh = [q_all[:, hh * SB_HEAD_DIM:(hh + 1) * SB_HEAD_DIM].astype(bf16) for hh in range(heads)]
    bias = [bias_ref[hp * heads + hh] * LOG2_E for hh in range(heads)]

    def key_tile(j, state, masked):
        start = pl.multiple_of(j * tile, tile)
        hs = range(heads)
        lanes = [slice(h * SB_HEAD_DIM, (h + 1) * SB_HEAD_DIM) for h in hs]
        kh = [k_ref[pl.ds(start, tile), lanes[h]].astype(bf16) for h in hs]
        z2 = [lax.dot_general(qh[h], kh[h], (((1,), (1,)), ((), ())), preferred_element_type=f32) + bias[h]
              for h in hs]
        sp2 = [jnp.maximum(z2[h], 0.0) + jnp.log2(1.0 + jnp.exp2(-jnp.abs(z2[h]))) for h in hs]
        drop = [jnp.where(causal, sp2[h], 0.0) if masked else sp2[h] for h in hs]
        inner = [jnp.dot(drop[h].astype(bf16), later, preferred_element_type=f32) for h in hs]
        a = [jnp.exp2(z2[h] - (sp2[h] + state[h][0] + inner[h])) for h in hs]
        if masked:
            a = [jnp.where(causal, a[h], 0.0) for h in hs]
        vh = [v_ref[pl.ds(start, tile), lanes[h]].astype(bf16) for h in hs]
        acc = [state[h][1] + jnp.dot(a[h].astype(bf16), vh[h], preferred_element_type=f32) for h in hs]
        carry = [state[h][0] + jnp.sum(drop[h], axis=1, keepdims=True) for h in hs]
        return tuple((carry[h], acc[h]) for h in hs)

    state = tuple((jnp.zeros((tile, 1), f32), jnp.zeros((tile, SB_HEAD_DIM), f32)) for _ in range(heads))
    state = key_tile(qi, state, True)
    state = lax.fori_loop(0, qi, lambda s, st: key_tile(qi - 1 - s, st, False), state)
    o_ref[...] = jnp.concatenate([acc for _, acc in state], axis=1)


def _sb_prompt(q, k, v, bias, tile=SB_TILE):
    b, s, w = q.shape
    assert s % tile == 0 and w % SB_STEP_WIDTH == 0
    blk = lambda bb, hp, qi: (bb, qi, hp)
    full = lambda bb, hp, qi: (bb, 0, hp)
    return pl.pallas_call(
        functools.partial(_sb_prompt_kernel, tile=tile),
        grid=(b, w // SB_STEP_WIDTH, s // tile),
        in_specs=[
            pl.BlockSpec(memory_space=pltpu.SMEM),
            pl.BlockSpec((None, tile, SB_STEP_WIDTH), blk),
            pl.BlockSpec((None, s, SB_STEP_WIDTH), full),
            pl.BlockSpec((None, s, SB_STEP_WIDTH), full),
        ],
        out_specs=pl.BlockSpec((None, tile, SB_STEP_WIDTH), blk),
        out_shape=jax.ShapeDtypeStruct((b, s, w), f32),
        compiler_params=pltpu.CompilerParams(
            dimension_semantics=("arbitrary", "arbitrary", "arbitrary"),
            vmem_limit_bytes=VMEM_LIMIT_BYTES),
        name="sb_prompt",
    )(bias, q, k, v)


SBS_PAGES_PER_STEP = 8


def _sb_sample_kernel(pt_ref, bias_ref, q_ref, *refs, n_slots):
    del pt_ref
    k_refs = refs[:n_slots]
    v_refs = refs[n_slots:2 * n_slots]
    o_ref = refs[2 * n_slots]
    qb_ref, acc_ref, carry_ref, z_ref = refs[2 * n_slots + 1:]
    step = pl.program_id(1)

    @pl.when(step == 0)
    def _():
        q = q_ref[...] * SB_SCALE
        qb_ref[...] = jnp.broadcast_to(q, qb_ref.shape)
        acc_ref[...] = jnp.zeros_like(acc_ref)
        carry_ref[...] = jnp.zeros_like(carry_ref)

    row = lax.broadcasted_iota(jnp.int32, (PAGE_SIZE, PAGE_SIZE), 0)
    col = lax.broadcasted_iota(jnp.int32, (PAGE_SIZE, PAGE_SIZE), 1)
    later = (row > col).astype(bf16)
    later2 = jnp.concatenate([later, later], axis=0)

    for i, k_ref in enumerate(k_refs):
        for h in range(SB_HEADS):
            zh = jnp.sum(k_ref[h] * qb_ref[h], axis=0, keepdims=True) + bias_ref[h]
            z_ref[pl.ds(i * SB_HEADS + h, 1), :] = zh
    z = z_ref[...]
    softplus = jnp.maximum(z, 0.0) + jnp.log(1.0 + jnp.exp(-jnp.abs(z)))
    log_keep = -softplus
    hi = log_keep.astype(bf16)
    lo = (log_keep - hi.astype(f32)).astype(bf16)
    inner = jnp.dot(jnp.concatenate([hi, lo], axis=1), later2, preferred_element_type=f32)
    total = jnp.sum(log_keep, axis=1, keepdims=True)
    base = z - softplus + inner
    carry = carry_ref[...]
    for i in range(n_slots):
        sl = slice(i * SB_HEADS, (i + 1) * SB_HEADS)
        w = jnp.exp(base[sl] + carry)
        for h in range(SB_HEADS):
            acc_ref[h] += v_refs[i][h] * w[h:h + 1, :]
        carry = carry + total[sl]
    carry_ref[...] = carry

    @pl.when(step == pl.num_programs(1) - 1)
    def _():
        o_ref[...] = jnp.sum(acc_ref[...], axis=2, keepdims=True)


def _sb_sample_past(q, bias, k_pool, v_pool, page_table):
    b, n_pages = page_table.shape
    n_slots = SBS_PAGES_PER_STEP
    assert n_pages % n_slots == 0
    page_block = (None, SB_HEADS, SB_HEAD_DIM, PAGE_SIZE)

    def page_spec(slot):
        return pl.BlockSpec(
            page_block, lambda bb, s, pt, slot=slot: (pt[bb, n_pages - 1 - (s * n_slots + slot)], 0, 0, 0))

    grid_spec = pltpu.PrefetchScalarGridSpec(
        num_scalar_prefetch=1,
        grid=(b, n_pages // n_slots),
        in_specs=[
            pl.BlockSpec(memory_space=pltpu.SMEM),
            pl.BlockSpec((None, SB_HEADS, SB_HEAD_DIM, 1), lambda bb, s, pt: (bb, 0, 0, 0)),
        ] + [page_spec(i) for i in range(n_slots)] * 2,
        out_specs=pl.BlockSpec((None, SB_HEADS, SB_HEAD_DIM, 1), lambda bb, s, pt: (bb, 0, 0, 0)),
        scratch_shapes=[
            pltpu.VMEM((SB_HEADS, SB_HEAD_DIM, PAGE_SIZE), f32),
            pltpu.VMEM((SB_HEADS, SB_HEAD_DIM, PAGE_SIZE), f32),
            pltpu.VMEM((SB_HEADS, 1), f32),
            pltpu.VMEM((n_slots * SB_HEADS, PAGE_SIZE), f32),
        ],
    )
    o = pl.pallas_call(
        functools.partial(_sb_sample_kernel, n_slots=n_slots),
        grid_spec=grid_spec,
        out_shape=jax.ShapeDtypeStruct((b, SB_HEADS, SB_HEAD_DIM, 1), f32),
        compiler_params=pltpu.CompilerParams(
            dimension_semantics=("arbitrary", "arbitrary"), vmem_limit_bytes=VMEM_LIMIT_BYTES),
        name="sb_sample",
    )(page_table, bias, q[..., None], *([k_pool] * n_slots), *([v_pool] * n_slots))
    return o[..., 0]


def _moe_kernel(be_ref, used_ref, x_ref, wgu_ref, bg_ref, bl_ref, wd_ref, bd_ref, rw_ref, o_ref,
                wg_s, wl_s, wd_s):
    i = pl.program_id(0)
    ff = wd_ref.shape[0]
    half = MXU_DIM // 2

    src = lax.broadcasted_iota(jnp.int32, (MXU_DIM, MXU_DIM), 0)
    dst = lax.broadcasted_iota(jnp.int32, (MXU_DIM, MXU_DIM), 1)
    unzip = (src == jnp.where(dst < half, 2 * dst, 2 * (dst - half) + 1)).astype(bf16)

    @pl.when((i == 0) | (be_ref[i] != be_ref[jnp.maximum(i - 1, 0)]))
    def _():
        for c in range(2 * ff // MXU_DIM):
            cols = slice(c * MXU_DIM, (c + 1) * MXU_DIM)
            out = slice(c * half, (c + 1) * half)
            w = jnp.dot(wgu_ref[:, cols].astype(bf16), unzip, preferred_element_type=f32).astype(bf16)
            wg_s[:, out] = w[:, :half]
            wl_s[:, out] = w[:, half:]
        wd_s[...] = wd_ref[...].astype(bf16)

    @pl.when(i < used_ref[0])
    def _():
        x = x_ref[...]
        g = jnp.dot(x, wg_s[...], preferred_element_type=f32) + bg_ref[...]
        l = jnp.dot(x, wl_s[...], preferred_element_type=f32) + bl_ref[...]
        glu = jnp.minimum(g, SWIGLU_LIMIT)
        lin = jnp.clip(l, -SWIGLU_LIMIT, SWIGLU_LIMIT)
        h = glu * (1.0 / (1.0 + jnp.exp(-SWIGLU_ALPHA * glu))) * (lin + 1.0)
        y = jnp.dot(h.astype(bf16), wd_s[...], preferred_element_type=f32) + bd_ref[...]
        o_ref[...] = (y * rw_ref[...]).astype(o_ref.dtype)

    @pl.when(i >= used_ref[0])
    def _():
        o_ref[...] = jnp.zeros_like(o_ref)


def _moe_experts(xb, block_e, n_used, w_gate_up, b_gate_up, w_down, b_down, row_w, tile_m):
    n_rows, d = xb.shape
    n_blocks = n_rows // tile_m
    ff = w_down.shape[1]
    row = lambda i, be, used: (i, 0)
    wsel = lambda i, be, used: (be[i], 0, 0)
    grid_spec = pltpu.PrefetchScalarGridSpec(
        num_scalar_prefetch=2,
        grid=(n_blocks,),
        in_specs=[
            pl.BlockSpec((tile_m, d), row),
            pl.BlockSpec((None, d, 2 * ff), wsel),
            pl.BlockSpec((None, 1, ff), wsel),
            pl.BlockSpec((None, 1, ff), wsel),
            pl.BlockSpec((None, ff, d), wsel),
            pl.BlockSpec((None, 1, d), wsel),
            pl.BlockSpec((tile_m, 1), row),
        ],
        out_specs=pl.BlockSpec((tile_m, d), row),
        scratch_shapes=[
            pltpu.VMEM((d, ff), bf16), pltpu.VMEM((d, ff), bf16), pltpu.VMEM((ff, d), bf16),
        ],
    )
    return pl.pallas_call(
        _moe_kernel,
        grid_spec=grid_spec,
        out_shape=jax.ShapeDtypeStruct((n_rows, d), bf16),
        compiler_params=pltpu.CompilerParams(
            dimension_semantics=("arbitrary",), vmem_limit_bytes=VMEM_LIMIT_BYTES),
        name="moe_experts",
    )(block_e, n_used, xb, w_gate_up, b_gate_up[:, None, 0::2], b_gate_up[:, None, 1::2], w_down,
      b_down[:, None, :], row_w)


def _moe(x, x_bf16, w_router, b_router, w_gate_up, b_gate_up, w_down, b_down, tile_m=MOE_TILE_M):
    t = x.shape[0]
    logits = jnp.dot(x, w_router, preferred_element_type=f32, precision=lax.Precision.HIGHEST) + b_router
    top_logit, top_e = lax.top_k(logits, TOP_K)
    top_p = jax.nn.softmax(top_logit, axis=-1)
    n_pairs = t * TOP_K
    n_blocks = -(-n_pairs // tile_m) + N_EXPERTS
    pair_e = top_e.reshape(-1).astype(jnp.int32)
    pair_id = jnp.arange(n_pairs, dtype=jnp.int32)
    e_sorted, order = lax.sort((pair_e, pair_id), num_keys=1, is_stable=True)
    _, pos = lax.sort((order, pair_id), num_keys=1)
    starts = jnp.searchsorted(e_sorted, jnp.arange(N_EXPERTS + 1, dtype=jnp.int32), side='left').astype(jnp.int32)
    counts = starts[1:] - starts[:-1]
    padded = (counts + tile_m - 1) // tile_m * tile_m
    pad_end = jnp.cumsum(padded)
    pad_start = pad_end - padded
    dest = pad_start[pair_e] + pos - starts[pair_e]
    block_e = jnp.minimum(
        jnp.searchsorted(pad_end, jnp.arange(n_blocks, dtype=jnp.int32) * tile_m, side='right'),
        N_EXPERTS - 1).astype(jnp.int32)
    n_used = (pad_end[-1:] // tile_m).astype(jnp.int32)
    n_rows = n_blocks * tile_m
    row_e = jnp.repeat(block_e, tile_m)
    off = jnp.arange(n_rows, dtype=jnp.int32) - pad_start[row_e]
    valid = off < counts[row_e]
    src = jnp.where(valid, order[jnp.clip(starts[row_e] + off, 0, n_pairs - 1)], 0)
    row_tok = src // TOP_K
    row_w = jnp.where(valid, top_p.reshape(-1)[src], 0.0)
    xb = x_bf16[row_tok]
    yb = _moe_experts(xb, block_e, n_used, w_gate_up, b_gate_up, w_down, b_down, row_w[:, None], tile_m)
    picked = lax.optimization_barrier(yb[dest])
    return picked.reshape(t, TOP_K * x.shape[1])


RW_LANE_HALF = LANES // 2
RW_KJ = RW_HEAD_DIM // 2
RW_VG = RW_HEAD_DIM // 8
RW_STEPS_PER_BLOCK = 32


def _rwkv_scan_kernel(kk_ref, d_ref, kka_ref, km_ref, r_ref, v_ref, s0_ref, y_ref, s_ref, *, steps):
    tb = pl.program_id(1)

    @pl.when(tb == 0)
    def _():
        s_ref[...] = s0_ref[...]

    def fold(x):
        return x + pltpu.roll(x, RW_LANE_HALF, axis=1)

    def step(t, carry):
        acc = [None] * RW_VG
        for j in range(RW_KJ):
            kkj = kk_ref[t, pl.ds(j, 1), :]
            for g in range(RW_VG):
                p = s_ref[g, j] * kkj
                acc[g] = p if acc[g] is None else acc[g] + p
        sa = [-fold(a) for a in acc]
        vv = [v_ref[t, g] for g in range(RW_VG)]
        yacc = [None] * RW_VG
        for j in range(RW_KJ):
            dj = d_ref[t, pl.ds(j, 1), :]
            kkaj = kka_ref[t, pl.ds(j, 1), :]
            kmj = km_ref[t, pl.ds(j, 1), :]
            rj = r_ref[t, pl.ds(j, 1), :]
            for g in range(RW_VG):
                s = s_ref[g, j] * dj + sa[g] * kkaj + vv[g] * kmj
                s_ref[g, j] = s
                p = s * rj
                yacc[g] = p if yacc[g] is None else yacc[g] + p
        for g in range(RW_VG):
            y_ref[t, g] = fold(yacc[g])
        return carry

    lax.fori_loop(0, steps, step, 0)


def _rwkv_scan(kk, d, kka, km, r, v, s0):
    n_g, s_len = kk.shape[:2]
    steps = min(RW_STEPS_PER_BLOCK, s_len)
    assert s_len % steps == 0
    krow = pl.BlockSpec((None, steps, RW_KJ, LANES), lambda g, t: (g, t, 0, 0))
    vrow = pl.BlockSpec((None, steps, RW_VG, 8, LANES), lambda g, t: (g, t, 0, 0, 0))
    st = pl.BlockSpec((None, RW_VG, RW_KJ, 8, LANES), lambda g, t: (g, 0, 0, 0, 0))
    return pl.pallas_call(
        functools.partial(_rwkv_scan_kernel, steps=steps),
        grid=(n_g, s_len // steps),
        in_specs=[krow, krow, krow, krow, krow, vrow, st],
        out_specs=[vrow, st],
        out_shape=[jax.ShapeDtypeStruct(v.shape, f32), jax.ShapeDtypeStruct(s0.shape, f32)],
        compiler_params=pltpu.CompilerParams(
            dimension_semantics=("arbitrary", "arbitrary"), vmem_limit_bytes=VMEM_LIMIT_BYTES),
        name="rwkv_scan",
    )(kk, d, kka, km, r, v, s0)


def _rw_groups(n_bh):
    assert n_bh % RW_LANE_HALF == 0
    return n_bh // RW_LANE_HALF


def _rw_k_layout(x):
    b, s, h, _ = x.shape
    g = _rw_groups(b * h)
    x = x.reshape(b, s, h, 2, RW_KJ)
    x = jnp.transpose(x, (1, 4, 3, 0, 2)).reshape(s, RW_KJ, 2, g, RW_LANE_HALF)
    return jnp.transpose(x, (3, 0, 1, 2, 4)).reshape(g, s, RW_KJ, LANES)


def _rw_v_layout(x):
    b, s, h, _ = x.shape
    g = _rw_groups(b * h)
    x = jnp.transpose(x, (1, 3, 0, 2)).reshape(s, RW_HEAD_DIM, g, RW_LANE_HALF)
    x = jnp.transpose(x, (2, 0, 1, 3))
    x = jnp.concatenate([x, x], axis=-1)
    return x.reshape(g, s, RW_VG, 8, LANES)


def _rw_v_unlayout(y, b, h):
    g, s = y.shape[:2]
    y = y.reshape(g, s, RW_HEAD_DIM, LANES)[..., :RW_LANE_HALF]
    y = jnp.transpose(y, (1, 2, 0, 3)).reshape(s, RW_HEAD_DIM, b, h)
    return jnp.transpose(y, (2, 0, 3, 1))


def _rw_state_layout(wkv):
    b, h = wkv.shape[:2]
    g = _rw_groups(b * h)
    x = wkv.reshape(g, RW_LANE_HALF, RW_VG, 8, 2, RW_KJ)
    return jnp.transpose(x, (0, 2, 5, 3, 4, 1)).reshape(g, RW_VG, RW_KJ, 8, LANES)


def _rw_state_unlayout(st, b, h):
    g = st.shape[0]
    x = st.reshape(g, RW_VG, RW_KJ, 8, 2, RW_LANE_HALF)
    x = jnp.transpose(x, (0, 5, 1, 3, 4, 2))
    return x.reshape(b, h, RW_HEAD_DIM, RW_HEAD_DIM)


MIX_TILE_M = 256


def _layer_norm_rows(pre, g, b):
    mu = jnp.mean(pre, axis=1, keepdims=True)
    cen = pre - mu
    var = jnp.mean(cen * cen, axis=1, keepdims=True)
    return cen * lax.rsqrt(var + LN_EPS) * g + b


def _sigmoid(x):
    return 1.0 / (1.0 + jnp.exp(-x))


def _mix_kernel(x_ref, osb_ref, qmem_ref, orw_ref, gsb_ref, gmem_ref, grw_ref, mk_ref, mv_ref,
                wsb_ref, wmem_ref, wrw_ref, wout_ref, g1_ref, b1_ref, h_ref, hb_ref):
    q = qmem_ref[...].astype(bf16)
    heads = []
    for hd in range(MEM_HEADS):
        lanes = slice(hd * MEM_HEAD_DIM, (hd + 1) * MEM_HEAD_DIM)
        s = lax.dot_general(q[:, lanes], mk_ref[:, lanes].astype(bf16), (((1,), (1,)), ((), ())),
                            preferred_element_type=f32) * MEM_SCALE
        p = jnp.exp(s - jnp.max(s, axis=1, keepdims=True))
        p = p / jnp.sum(p, axis=1, keepdims=True)
        heads.append(jnp.dot(p.astype(bf16), mv_ref[:, lanes].astype(bf16), preferred_element_type=f32))
    o_mem = jnp.concatenate(heads, axis=1)
    branch = lambda o, w_ref: jnp.dot(o.astype(bf16), w_ref[...], preferred_element_type=f32)
    merged = (_sigmoid(gsb_ref[...]) * branch(osb_ref[...], wsb_ref)
              + _sigmoid(gmem_ref[...]) * branch(o_mem, wmem_ref)
              + _sigmoid(grw_ref[...]) * branch(orw_ref[...], wrw_ref))
    pre = DN_ALPHA * x_ref[...] + branch(merged, wout_ref)
    h = _layer_norm_rows(pre, g1_ref[...], b1_ref[...])
    h_ref[...] = h
    hb_ref[...] = h.astype(bf16)


def _mix(x, o_sb, q_mem, o_rw, g_sb, g_mem, g_rw, mem_k, mem_v, w_sb_o, w_mem_o, w_rw_o, w_out, ln_g, ln_b):
    b, s, d = x.shape
    tile = min(MIX_TILE_M, s)
    assert s % tile == 0
    tok = lambda w: pl.BlockSpec((None, tile, w), lambda bb, i: (bb, i, 0))
    mem = pl.BlockSpec((None, N_MEM, MEM_WIDTH), lambda bb, i: (bb, 0, 0))
    const = lambda a: pl.BlockSpec(a.shape, lambda bb, i: (0,) * a.ndim)
    weights = [w.astype(bf16) for w in (w_sb_o, w_mem_o, w_rw_o, w_out)] + [ln_g[None, :], ln_b[None, :]]
    return pl.pallas_call(
        _mix_kernel,
        grid=(b, s // tile),
        in_specs=[tok(d), tok(SB_WIDTH), tok(MEM_WIDTH), tok(RW_WIDTH), tok(d), tok(d), tok(d), mem, mem]
        + [const(w) for w in weights],
        out_specs=[tok(d), tok(d)],
        out_shape=[jax.ShapeDtypeStruct((b, s, d), f32), jax.ShapeDtypeStruct((b, s, d), bf16)],
        compiler_params=pltpu.CompilerParams(
            dimension_semantics=("arbitrary", "arbitrary"), vmem_limit_bytes=VMEM_LIMIT_BYTES),
        name="mix_ln1",
    )(x, o_sb, q_mem, o_rw, g_sb, g_mem, g_rw, mem_k, mem_v, *weights)


def _combine_kernel(p_ref, h_ref, g_ref, b_ref, y_ref):
    d = h_ref.shape[1]
    moe = p_ref[:, 0:d].astype(f32)
    for j in range(1, TOP_K):
        moe = moe + p_ref[:, j * d:(j + 1) * d].astype(f32)
    y_ref[...] = _layer_norm_rows(DN_ALPHA * h_ref[...] + moe, g_ref[...], b_ref[...])


def _combine_rows_tile(t):
    return max(m for m in range(8, 513, 8) if t % m == 0)


def _combine(picked, h, ln_g, ln_b):
    t, d = h.shape
    tile = _combine_rows_tile(t)
    row = lambda w: pl.BlockSpec((tile, w), lambda i: (i, 0))
    vec = pl.BlockSpec((1, d), lambda i: (0, 0))
    return pl.pallas_call(
        _combine_kernel,
        grid=(t // tile,),
        in_specs=[row(TOP_K * d), row(d), vec, vec],
        out_specs=row(d),
        out_shape=jax.ShapeDtypeStruct((t, d), f32),
        compiler_params=pltpu.CompilerParams(
            dimension_semantics=("arbitrary",), vmem_limit_bytes=VMEM_LIMIT_BYTES),
        name="combine_ln2",
    )(picked, h, ln_g[None, :], ln_b[None, :])


def _sb_logits(q, k, bias):
    z = jnp.einsum('bqhd,bkhd->bhqk', q, k, preferred_element_type=f32) * SB_SCALE
    return z + bias.astype(f32)[None, :, None, None]


def _sb_weights(z, causal):
    log_keep = jnp.where(causal, jax.nn.log_sigmoid(-z), 0.0)
    log_survive = lax.cumsum(log_keep, axis=z.ndim - 1, reverse=True) - log_keep
    return jnp.where(causal, jnp.exp(jax.nn.log_sigmoid(z) + log_survive), 0.0)


def _sb_sample(q, k_new, v_new, bias, k_pool, v_pool, page_table):
    b, s = q.shape[:2]
    assert s == 1
    to_token_minor = lambda pool: jnp.transpose(pool, (0, 2, 3, 1))
    o_past = _sb_sample_past(q[:, 0], bias, to_token_minor(k_pool), to_token_minor(v_pool), page_table)
    new_pos = jnp.arange(s)
    w_new = _sb_weights(_sb_logits(q, k_new, bias), new_pos[None, :] < new_pos[:, None])
    o_new = jnp.einsum('bhqk,bkhd->bqhd', w_new, v_new)
    return (o_past[:, None] + o_new).reshape(b, s, SB_WIDTH)


def _rwkv7(p_rw, shift0, wkv0, mu_rw, w_decay0, w_decay2, w_aaa0, w_aaa2, w_gate2,
           rw_k_k, rw_k_a, rw_r_k, rw_gn_g, rw_gn_b):
    b, s, _ = p_rw.shape
    prev = jnp.concatenate([shift0[:, None, :], p_rw[:, :-1]], axis=1)
    xs = p_rw + mu_rw * (prev - p_rw)
    r, k, v, w_lo, a_lo, g_lo = _split(xs, RW_SPLITS)
    w_raw = w_decay0 + jnp.tanh(w_lo) @ w_decay2
    decay = jnp.exp(-jnp.exp(-jax.nn.softplus(-w_raw) - 0.5))
    a = jax.nn.sigmoid(w_aaa0 + a_lo @ w_aaa2)
    g = jax.nn.sigmoid(g_lo) @ w_gate2
    heads = lambda t: t.reshape(b, s, RW_HEADS, RW_HEAD_DIM)
    kk = heads(k * rw_k_k)
    kk = kk / jnp.maximum(jnp.sqrt(jnp.sum(kk * kk, axis=-1, keepdims=True)), 1e-12)
    k_mod = k * (1.0 + (a - 1.0) * rw_k_a)
    rh, kh, vh, ah, dh = heads(r), heads(k_mod), heads(v), heads(a), heads(decay)

    y, wkv = _rwkv_scan(_rw_k_layout(kk), _rw_k_layout(dh), _rw_k_layout(kk * ah), _rw_k_layout(kh),
                        _rw_k_layout(rh), _rw_v_layout(vh), _rw_state_layout(wkv0))
    y = _rw_v_unlayout(y, b, RW_HEADS)
    wkv = _rw_state_unlayout(wkv, b, RW_HEADS)
    mu = y.mean(-1, keepdims=True)
    var = jnp.square(y - mu).mean(-1, keepdims=True)
    y = ((y - mu) * lax.rsqrt(var + RW_GN_EPS)).reshape(b, s, RW_WIDTH) * rw_gn_g + rw_gn_b
    bonus = jnp.sum(rh * kh * rw_r_k, axis=-1, keepdims=True) * vh
    y = (y + bonus.reshape(b, s, RW_WIDTH)) * g
    return y, p_rw[:, -1], wkv


def _branch_mix(x, parts, o_sb, mem_k, mem_v, shift0, wkv0, lw):
    q_mem, p_rw, g_sb, g_mem, g_rw = parts
    o_rw, shift, wkv = _rwkv7(p_rw, shift0, wkv0, lw['mu_rw'], lw['w_decay0'], lw['w_decay2'],
                              lw['w_aaa0'], lw['w_aaa2'], lw['w_gate2'], lw['rw_k_k'], lw['rw_k_a'],
                              lw['rw_r_k'], lw['rw_gn_g'], lw['rw_gn_b'])
    h, h_bf16 = _mix(x, o_sb, q_mem, o_rw, g_sb, g_mem, g_rw, mem_k, mem_v, lw['w_sb_o'], lw['w_mem_o'],
                     lw['w_rw_o'], lw['w_out'], lw['ln1_g'], lw['ln1_b'])
    return h, h_bf16, shift, wkv


def kernel(x_prompt, x_sample, cache_sb_k, cache_sb_v, cache_mem_k, cache_mem_v, state_rw_shift,
           state_rw_wkv, page_table, mem_prompt, w_in, sb_bias, mu_rw, w_decay0, w_decay2, w_aaa0,
           w_aaa2, w_gate2, rw_k_k, rw_k_a, rw_r_k, rw_gn_g, rw_gn_b, w_mem_kv, w_sb_o, w_mem_o,
           w_rw_o, w_out, ln1_g, ln1_b, w_router, b_router, w_gate_up, b_gate_up, w_down, b_down,
           ln2_g, ln2_b):
    assert w_in.shape[0] == DEPTH == 1
    l = 0
    lw = dict(mu_rw=mu_rw[l], w_decay0=w_decay0[l], w_decay2=w_decay2[l], w_aaa0=w_aaa0[l],
              w_aaa2=w_aaa2[l], w_gate2=w_gate2[l], rw_k_k=rw_k_k[l], rw_k_a=rw_k_a[l],
              rw_r_k=rw_r_k[l], rw_gn_g=rw_gn_g[l], rw_gn_b=rw_gn_b[l], w_sb_o=w_sb_o[l],
              w_mem_o=w_mem_o[l], w_rw_o=w_rw_o[l], w_out=w_out[l], ln1_g=ln1_g[l], ln1_b=ln1_b[l])
    bp, sp, _ = x_prompt.shape
    bs, ss, _ = x_sample.shape
    w_in_b = w_in[l].astype(bf16)

    xp = x_prompt.reshape(bp * sp, D_MODEL)
    q_sb, k_sb, v_sb, *rest = _proj(xp, w_in_b, IN_SPLITS, PROJ_TILE_M)
    tok3 = lambda a: a.reshape(bp, sp, a.shape[-1])
    q_sb, k_sb, v_sb = tok3(q_sb), tok3(k_sb), tok3(v_sb)
    o_sb = _sb_prompt(q_sb, k_sb, v_sb, sb_bias[l])
    mem_k, mem_v = _proj(mem_prompt.reshape(bp * N_MEM, D_MODEL), w_mem_kv[l].astype(bf16),
                         (MEM_WIDTH, MEM_WIDTH), PROJ_TILE_M)
    mem_k = mem_k.reshape(bp, N_MEM, MEM_WIDTH)
    mem_v = mem_v.reshape(bp, N_MEM, MEM_WIDTH)
    h_p, hb_p, shift_p, wkv_p = _branch_mix(
        x_prompt, [tok3(a) for a in rest], o_sb, mem_k, mem_v, jnp.zeros((bp, RW_SHIFT_WIDTH), f32),
        jnp.zeros((bp, RW_HEADS, RW_HEAD_DIM, RW_HEAD_DIM), f32), lw)

    xs = x_sample.reshape(bs * ss, D_MODEL)
    q_s, k_s, v_s, *rest_s = _proj(xs, w_in_b, IN_SPLITS, bs * ss)
    tok3s = lambda a: a.reshape(bs, ss, a.shape[-1])
    heads_s = lambda a: a.reshape(bs, ss, SB_HEADS, SB_HEAD_DIM)
    o_sb_s = _sb_sample(heads_s(q_s), heads_s(k_s), heads_s(v_s), sb_bias[l], cache_sb_k[l], cache_sb_v[l],
                        page_table)
    h_s, hb_s, shift_s, wkv_s = _branch_mix(
        x_sample, [tok3s(a) for a in rest_s], o_sb_s, cache_mem_k[l].reshape(bs, N_MEM, MEM_WIDTH),
        cache_mem_v[l].reshape(bs, N_MEM, MEM_WIDTH), state_rw_shift[l], state_rw_wkv[l], lw)

    flat = lambda hp_, hs_: jnp.concatenate([hp_.reshape(bp * sp, D_MODEL), hs_.reshape(bs * ss, D_MODEL)], axis=0)
    h_all = flat(h_p, h_s)
    picked = _moe(h_all, flat(hb_p, hb_s), w_router[l], b_router[l], w_gate_up[l], b_gate_up[l], w_down[l],
                  b_down[l])
    y_all = _combine(picked, h_all, ln2_g[l], ln2_b[l])
    y_prompt = y_all[:bp * sp].reshape(bp, sp, D_MODEL)
    y_sample = y_all[bp * sp:].reshape(bs, ss, D_MODEL)

    heads_m = lambda a: a.reshape(1, bp, N_MEM, MEM_HEADS, MEM_HEAD_DIM)
    return (y_prompt, y_sample,
            k_sb.reshape(1, bp, sp, SB_HEADS, SB_HEAD_DIM), v_sb.reshape(1, bp, sp, SB_HEADS, SB_HEAD_DIM),
            heads_m(mem_k), heads_m(mem_v), shift_p[None], wkv_p[None],
            k_s.reshape(1, bs, ss, SB_HEADS, SB_HEAD_DIM), v_s.reshape(1, bs, ss, SB_HEADS, SB_HEAD_DIM),
            shift_s[None], wkv_s[None])
```

```python
import functools

import jax
import jax.numpy as jnp
import numpy as np
from jax import lax
from jax.experimental import pallas as pl
from jax.experimental.pallas import tpu as pltpu

D_MODEL = 1024
PAGE_SIZE = 128
N_MEM = 256
SB_HEADS = 8
SB_HEAD_DIM = 64
SB_WIDTH = SB_HEADS * SB_HEAD_DIM
SB_SCALE = SB_HEAD_DIM ** -0.5
LOG2_E = 1.4426950408889634
MEM_HEADS = 4
MEM_HEAD_DIM = 128
MEM_WIDTH = MEM_HEADS * MEM_HEAD_DIM
MEM_SCALE = MEM_HEAD_DIM ** -0.5
RW_HEADS = 8
RW_HEAD_DIM = 64
RW_WIDTH = RW_HEADS * RW_HEAD_DIM
DECAY_LORA = 64
AAA_LORA = 64
GATE_LORA = 128
RW_SPLITS = (RW_WIDTH, RW_WIDTH, RW_WIDTH, DECAY_LORA, AAA_LORA, GATE_LORA)
RW_SHIFT_WIDTH = sum(RW_SPLITS)
RW_GN_EPS = 64e-5
N_EXPERTS = 32
TOP_K = 4
D_FF = D_MODEL
SWIGLU_ALPHA = 1.702
SWIGLU_LIMIT = 7.0
DEPTH = 1
DN_ALPHA = (2 * DEPTH) ** 0.25
LN_EPS = 1e-5
IN_SPLITS = (SB_WIDTH, SB_WIDTH, SB_WIDTH, MEM_WIDTH, RW_SHIFT_WIDTH, D_MODEL, D_MODEL, D_MODEL)
IN_WIDTH = sum(IN_SPLITS)

LANES = 128
MXU_DIM = 256
VMEM_LIMIT_BYTES = 48 * 1024 * 1024

SB_TILE = MXU_DIM
SB_STEP_WIDTH = 4 * SB_HEAD_DIM
PROJ_TILE_M = 256
MOE_TILE_M = 256

f32 = jnp.float32
bf16 = jnp.bfloat16


def _split(x, sizes):
    return jnp.split(x, np.cumsum(sizes)[:-1].tolist(), axis=-1)


def _proj_kernel(x_ref, w_ref, *o_refs, splits):
    xb = x_ref[...].astype(bf16)
    off = 0
    for o_ref, n in zip(o_refs, splits):
        o_ref[...] = jnp.dot(xb, w_ref[:, off:off + n], preferred_element_type=f32)
        off += n


def _proj(x, w_bf16, splits, tile_m):
    t, d = x.shape
    n = w_bf16.shape[1]
    assert t % tile_m == 0 and sum(splits) == n
    return pl.pallas_call(
        functools.partial(_proj_kernel, splits=splits),
        grid=(t // tile_m,),
        in_specs=[
            pl.BlockSpec((tile_m, d), lambda i: (i, 0)),
            pl.BlockSpec((d, n), lambda i: (0, 0), pipeline_mode=pl.Buffered(1)),
        ],
        out_specs=[pl.BlockSpec((tile_m, s), lambda i: (i, 0)) for s in splits],
        out_shape=[jax.ShapeDtypeStruct((t, s), f32) for s in splits],
        compiler_params=pltpu.CompilerParams(
            dimension_semantics=("arbitrary",), vmem_limit_bytes=VMEM_LIMIT_BYTES),
        name="proj_in",
    )(x, w_bf16)


def _sb_prompt_kernel(bias_ref, q_ref, k_ref, v_ref, o_ref, *, tile):
    hp = pl.program_id(1)
    qi = pl.program_id(2)
    heads = q_ref.shape[-1] // SB_HEAD_DIM
    row = lax.broadcasted_iota(jnp.int32, (tile, tile), 0)
    col = lax.broadcasted_iota(jnp.int32, (tile, tile), 1)
    later = (row > col).astype(bf16)
    causal = col < row
    q_all = q_ref[...] * (SB_SCALE * LOG2_E)
    qh = [q_all[:, hh * SB_HEAD_DIM:(hh + 1) * SB_HEAD_DIM].astype(bf16) for hh in range(heads)]
    bias = [bias_ref[hp * heads + hh] * LOG2_E for hh in range(heads)]

    def key_tile(j, state, masked):
        start = pl.multiple_of(j * tile, tile)
        hs = range(heads)
        lanes = [slice(h * SB_HEAD_DIM, (h + 1) * SB_HEAD_DIM) for h in hs]
        kh = [k_ref[pl.ds(start, tile), lanes[h]].astype(bf16) for h in hs]
        z2 = [lax.dot_general(qh[h], kh[h], (((1,), (1,)), ((), ())), preferred_element_type=f32) + bias[h]
              for h in hs]
        sp2 = [jnp.maximum(z2[h], 0.0) + jnp.log2(1.0 + jnp.exp2(-jnp.abs(z2[h]))) for h in hs]
        drop = [jnp.where(causal, sp2[h], 0.0) if masked else sp2[h] for h in hs]
        inner = [jnp.dot(drop[h].astype(bf16), later, preferred_element_type=f32) for h in hs]
        a = [jnp.exp2(z2[h] - (sp2[h] + state[h][0] + inner[h])) for h in hs]
        if masked:
            a = [jnp.where(causal, a[h], 0.0) for h in hs]
        vh = [v_ref[pl.ds(start, tile), lanes[h]].astype(bf16) for h in hs]
        acc = [state[h][1] + jnp.dot(a[h].astype(bf16), vh[h], preferred_element_type=f32) for h in hs]
        carry = [state[h][0] + jnp.sum(drop[h], axis=1, keepdims=True) for h in hs]
        return tuple((carry[h], acc[h]) for h in hs)

    state = tuple((jnp.zeros((tile, 1), f32), jnp.zeros((tile, SB_HEAD_DIM), f32)) for _ in range(heads))
    state = key_tile(qi, state, True)
    state = lax.fori_loop(0, qi, lambda s, st: key_tile(qi - 1 - s, st, False), state)
    o_ref[...] = jnp.concatenate([acc for _, acc in state], axis=1)


def _sb_prompt(q, k, v, bias, tile=SB_TILE):
    b, s, w = q.shape
    assert s % tile == 0 and w % SB_STEP_WIDTH == 0
    blk = lambda bb, hp, qi: (bb, qi, hp)
    full = lambda bb, hp, qi: (bb, 0, hp)
    return pl.pallas_call(
        functools.partial(_sb_prompt_kernel, tile=tile),
        grid=(b, w // SB_STEP_WIDTH, s // tile),
        in_specs=[
            pl.BlockSpec(memory_space=pltpu.SMEM),
            pl.BlockSpec((None, tile, SB_STEP_WIDTH), blk),
            pl.BlockSpec((None, s, SB_STEP_WIDTH), full),
            pl.BlockSpec((None, s, SB_STEP_WIDTH), full),
        ],
        out_specs=pl.BlockSpec((None, tile, SB_STEP_WIDTH), blk),
        out_shape=jax.ShapeDtypeStruct((b, s, w), f32),
        compiler_params=pltpu.CompilerParams(
            dimension_semantics=("arbitrary", "arbitrary", "arbitrary"),
            vmem_limit_bytes=VMEM_LIMIT_BYTES),
        name="sb_prompt",
    )(bias, q, k, v)


SBS_PAGES_PER_STEP = 8


def _sb_sample_kernel(pt_ref, bias_ref, q_ref, *refs, n_slots):
    del pt_ref
    k_refs = refs[:n_slots]
    v_refs = refs[n_slots:2 * n_slots]
    o_ref = refs[2 * n_slots]
    qb_ref, acc_ref, carry_ref, z_ref = refs[2 * n_slots + 1:]
    step = pl.program_id(1)

    @pl.when(step == 0)
    def _():
        q = q_ref[...] * SB_SCALE
        qb_ref[...] = jnp.broadcast_to(q, qb_ref.shape)
        acc_ref[...] = jnp.zeros_like(acc_ref)
        carry_ref[...] = jnp.zeros_like(carry_ref)

    row = lax.broadcasted_iota(jnp.int32, (PAGE_SIZE, PAGE_SIZE), 0)
    col = lax.broadcasted_iota(jnp.int32, (PAGE_SIZE, PAGE_SIZE), 1)
    later = (row > col).astype(bf16)
    later2 = jnp.concatenate([later, later], axis=0)

    for i, k_ref in enumerate(k_refs):
        for h in range(SB_HEADS):
            zh = jnp.sum(k_ref[h] * qb_ref[h], axis=0, keepdims=True) + bias_ref[h]
            z_ref[pl.ds(i * SB_HEADS + h, 1), :] = zh
    z = z_ref[...]
    softplus = jnp.maximum(z, 0.0) + jnp.log(1.0 + jnp.exp(-jnp.abs(z)))
    log_keep = -softplus
    hi = log_keep.astype(bf16)
    lo = (log_keep - hi.astype(f32)).astype(bf16)
    inner = jnp.dot(jnp.concatenate([hi, lo], axis=1), later2, preferred_element_type=f32)
    total = jnp.sum(log_keep, axis=1, keepdims=True)
    base = z - softplus + inner
    carry = carry_ref[...]
    for i in range(n_slots):
        sl = slice(i * SB_HEADS, (i + 1) * SB_HEADS)
        w = jnp.exp(base[sl] + carry)
        for h in range(SB_HEADS):
            acc_ref[h] += v_refs[i][h] * w[h:h + 1, :]
        carry = carry + total[sl]
    carry_ref[...] = carry

    @pl.when(step == pl.num_programs(1) - 1)
    def _():
        o_ref[...] = jnp.sum(acc_ref[...], axis=2, keepdims=True)


def _sb_sample_past(q, bias, k_pool, v_pool, page_table):
    b, n_pages = page_table.shape
    n_slots = SBS_PAGES_PER_STEP
    assert n_pages % n_slots == 0
    page_block = (None, SB_HEADS, SB_HEAD_DIM, PAGE_SIZE)

    def page_spec(slot):
        return pl.BlockSpec(
            page_block, lambda bb, s, pt, slot=slot: (pt[bb, n_pages - 1 - (s * n_slots + slot)], 0, 0, 0))

    grid_spec = pltpu.PrefetchScalarGridSpec(
        num_scalar_prefetch=1,
        grid=(b, n_pages // n_slots),
        in_specs=[
            pl.BlockSpec(memory_space=pltpu.SMEM),
            pl.BlockSpec((None, SB_HEADS, SB_HEAD_DIM, 1), lambda bb, s, pt: (bb, 0, 0, 0)),
        ] + [page_spec(i) for i in range(n_slots)] * 2,
        out_specs=pl.BlockSpec((None, SB_HEADS, SB_HEAD_DIM, 1), lambda bb, s, pt: (bb, 0, 0, 0)),
        scratch_shapes=[
            pltpu.VMEM((SB_HEADS, SB_HEAD_DIM, PAGE_SIZE), f32),
            pltpu.VMEM((SB_HEADS, SB_HEAD_DIM, PAGE_SIZE), f32),
            pltpu.VMEM((SB_HEADS, 1), f32),
            pltpu.VMEM((n_slots * SB_HEADS, PAGE_SIZE), f32),
        ],
    )
    o = pl.pallas_call(
        functools.partial(_sb_sample_kernel, n_slots=n_slots),
        grid_spec=grid_spec,
        out_shape=jax.ShapeDtypeStruct((b, SB_HEADS, SB_HEAD_DIM, 1), f32),
        compiler_params=pltpu.CompilerParams(
            dimension_semantics=("arbitrary", "arbitrary"), vmem_limit_bytes=VMEM_LIMIT_BYTES),
        name="sb_sample",
    )(page_table, bias, q[..., None], *([k_pool] * n_slots), *([v_pool] * n_slots))
    return o[..., 0]


def _token_tile(t):
    return max(m for m in range(8, 513, 8) if t % m == 0)


def _route_kernel(h_ref, whi_ref, wlo_ref, b_ref, e_ref, p_ref, cnt_ref, hb_ref):
    h = h_ref[...]
    h_hi = h.astype(bf16)
    hb_ref[...] = h_hi
    h_lo = (h - h_hi.astype(f32)).astype(bf16)
    dot = lambda a, w_ref: jnp.dot(a, w_ref[...], preferred_element_type=f32)
    work = dot(h_hi, whi_ref) + (dot(h_hi, wlo_ref) + dot(h_lo, whi_ref)) + b_ref[...]
    lane = lax.broadcasted_iota(jnp.int32, work.shape, 1)
    picked = jnp.zeros(work.shape, jnp.bool_)
    top_e, top_l = [], []
    for _ in range(TOP_K):
        m = jnp.max(work, axis=1, keepdims=True)
        idx = jnp.min(jnp.where(work == m, lane, N_EXPERTS), axis=1, keepdims=True)
        hit = lane == idx
        picked = picked | hit
        work = jnp.where(hit, -jnp.inf, work)
        top_e.append(idx)
        top_l.append(m)
    ex = [jnp.exp(l - top_l[0]) for l in top_l]
    den = ex[0] + ex[1] + ex[2] + ex[3]
    e_ref[...] = jnp.concatenate(top_e, axis=1)
    p_ref[...] = jnp.concatenate([x / den for x in ex], axis=1)
    cnt_ref[...] = jnp.sum(picked.astype(f32), axis=0, keepdims=True).astype(jnp.int32)


def _route(h, w_router, b_router):
    t, d = h.shape
    tile = _token_tile(t)
    w_hi = w_router.astype(bf16)
    w_lo = (w_router - w_hi.astype(f32)).astype(bf16)
    const = lambda a: pl.BlockSpec(a.shape, lambda i: (0,) * a.ndim)
    b2 = b_router[None, :]
    return pl.pallas_call(
        _route_kernel,
        grid=(t // tile,),
        in_specs=[pl.BlockSpec((tile, d), lambda i: (i, 0)), const(w_hi), const(w_lo), const(b2)],
        out_specs=[pl.BlockSpec((tile, TOP_K), lambda i: (i, 0)), pl.BlockSpec((tile, TOP_K), lambda i: (i, 0)),
                   pl.BlockSpec((None, 1, N_EXPERTS), lambda i: (i, 0, 0)), pl.BlockSpec((tile, d), lambda i: (i, 0))],
        out_shape=[jax.ShapeDtypeStruct((t, TOP_K), jnp.int32), jax.ShapeDtypeStruct((t, TOP_K), f32),
                   jax.ShapeDtypeStruct((t // tile, 1, N_EXPERTS), jnp.int32), jax.ShapeDtypeStruct((t, d), bf16)],
        compiler_params=pltpu.CompilerParams(
            dimension_semantics=("arbitrary",), vmem_limit_bytes=VMEM_LIMIT_BYTES),
        name="moe_route",
    )(h, w_hi, w_lo, b2)


def _pair_pos_kernel(e_ref, base_ref, pos_ref):
    tile = e_ref.shape[0]
    lane = lax.broadcasted_iota(jnp.int32, (tile, N_EXPERTS), 1)
    hits = [lane == e_ref[:, j:j + 1] for j in range(TOP_K)]
    any_hit = hits[0] | hits[1] | hits[2] | hits[3]
    row = lax.broadcasted_iota(jnp.int32, (tile, tile), 0)
    col = lax.broadcasted_iota(jnp.int32, (tile, tile), 1)
    earlier = (col < row).astype(bf16)
    before = jnp.dot(earlier, any_hit.astype(bf16), preferred_element_type=f32) + base_ref[...].astype(f32)
    pos = [jnp.sum(jnp.where(hit, before, 0.0), axis=1, keepdims=True) for hit in hits]
    pos_ref[...] = jnp.concatenate(pos, axis=1).astype(jnp.int32)


def _pair_pos(top_e, tile_base):
    t = top_e.shape[0]
    n_tiles = tile_base.shape[0]
    tile = t // n_tiles
    return pl.pallas_call(
        _pair_pos_kernel,
        grid=(n_tiles,),
        in_specs=[pl.BlockSpec((tile, TOP_K), lambda i: (i, 0)),
                  pl.BlockSpec((None, 1, N_EXPERTS), lambda i: (i, 0, 0))],
        out_specs=pl.BlockSpec((tile, TOP_K), lambda i: (i, 0)),
        out_shape=jax.ShapeDtypeStruct((t, TOP_K), jnp.int32),
        compiler_params=pltpu.CompilerParams(
            dimension_semantics=("arbitrary",), vmem_limit_bytes=VMEM_LIMIT_BYTES),
        name="moe_pair_pos",
    )(top_e, tile_base)


def _moe_kernel(tile_ref, exp_ref, lo_ref, hi_ref, first_ref, x_ref, wgu_ref, bg_ref, bl_ref, wd_ref, bd_ref,
                o_ref, wg_s, wl_s, wd_s):
    del tile_ref
    i = pl.program_id(0)
    ff = wd_ref.shape[0]
    half = MXU_DIM // 2

    src = lax.broadcasted_iota(jnp.int32, (MXU_DIM, MXU_DIM), 0)
    dst = lax.broadcasted_iota(jnp.int32, (MXU_DIM, MXU_DIM), 1)
    unzip = (src == jnp.where(dst < half, 2 * dst, 2 * (dst - half) + 1)).astype(bf16)

    @pl.when((i == 0) | (exp_ref[i] != exp_ref[jnp.maximum(i - 1, 0)]))
    def _():
        for c in range(2 * ff // MXU_DIM):
            cols = slice(c * MXU_DIM, (c + 1) * MXU_DIM)
            out = slice(c * half, (c + 1) * half)
            w = jnp.dot(wgu_ref[:, cols].astype(bf16), unzip, preferred_element_type=f32).astype(bf16)
            wg_s[:, out] = w[:, :half]
            wl_s[:, out] = w[:, half:]
        wd_s[...] = wd_ref[...].astype(bf16)

    @pl.when(first_ref[i] == 1)
    def _():
        o_ref[...] = jnp.zeros_like(o_ref)

    @pl.when(hi_ref[i] > lo_ref[i])
    def _():
        x = x_ref[...]
        g = jnp.dot(x, wg_s[...], preferred_element_type=f32) + bg_ref[...]
        l = jnp.dot(x, wl_s[...], preferred_element_type=f32) + bl_ref[...]
        glu = jnp.minimum(g, SWIGLU_LIMIT)
        lin = jnp.clip(l, -SWIGLU_LIMIT, SWIGLU_LIMIT)
        h = glu * (1.0 / (1.0 + jnp.exp(-SWIGLU_ALPHA * glu))) * (lin + 1.0)
        y = jnp.dot(h.astype(bf16), wd_s[...], preferred_element_type=f32) + bd_ref[...]
        row = lax.broadcasted_iota(jnp.int32, (x.shape[0], 1), 0)
        mine = (row >= lo_ref[i]) & (row < hi_ref[i])
        o_ref[...] = jnp.where(mine, y.astype(o_ref.dtype), o_ref[...])


def _moe_experts(xs, items, w_gate_up, b_gate_up, w_down, b_down, tile_m):
    n_rows, d = xs.shape
    ff = w_down.shape[1]
    n_items = items[0].shape[0]
    row = lambda i, tl, ex, lo, hi, fi: (tl[i], 0)
    wsel = lambda i, tl, ex, lo, hi, fi: (ex[i], 0, 0)
    grid_spec = pltpu.PrefetchScalarGridSpec(
        num_scalar_prefetch=5,
        grid=(n_items,),
        in_specs=[
            pl.BlockSpec((tile_m, d), row),
            pl.BlockSpec((None, d, 2 * ff), wsel),
            pl.BlockSpec((None, 1, ff), wsel),
            pl.BlockSpec((None, 1, ff), wsel),
            pl.BlockSpec((None, ff, d), wsel),
            pl.BlockSpec((None, 1, d), wsel),
        ],
        out_specs=pl.BlockSpec((tile_m, d), row),
        scratch_shapes=[
            pltpu.VMEM((d, ff), bf16), pltpu.VMEM((d, ff), bf16), pltpu.VMEM((ff, d), bf16),
        ],
    )
    return pl.pallas_call(
        _moe_kernel,
        grid_spec=grid_spec,
        out_shape=jax.ShapeDtypeStruct((n_rows, d), bf16),
        compiler_params=pltpu.CompilerParams(
            dimension_semantics=("arbitrary",), vmem_limit_bytes=VMEM_LIMIT_BYTES),
        name="moe_experts",
    )(*items, xs, w_gate_up, b_gate_up[:, None, 0::2], b_gate_up[:, None, 1::2], w_down, b_down[:, None, :])


def _moe_work_items(starts, n_tiles, tile_m):
    counts = starts[1:] - starts[:-1]
    first_tile = starts[:-1] // tile_m
    n_t = jnp.where(counts > 0, (starts[1:] - 1) // tile_m - first_tile + 1, 0)
    item_end = jnp.cumsum(n_t)
    item_start = item_end - n_t
    n_items = n_tiles + N_EXPERTS - 1
    i = jnp.arange(n_items, dtype=jnp.int32)
    valid = i < item_end[-1]
    ii = jnp.minimum(i, item_end[-1] - 1)
    exp = jnp.sum((ii[:, None] >= item_end[None, :]).astype(jnp.int32), axis=1)
    onehot = (exp[:, None] == jnp.arange(N_EXPERTS, dtype=jnp.int32)[None, :]).astype(jnp.int32)
    pick = lambda table: jnp.sum(onehot * table[None, :], axis=1)
    tile = pick(first_tile) + ii - pick(item_start)
    lo = jnp.clip(pick(starts[:-1]) - tile * tile_m, 0, tile_m)
    hi = jnp.clip(pick(starts[1:]) - tile * tile_m, 0, tile_m)
    hi = jnp.where(valid, hi, lo)
    first = jnp.concatenate([jnp.ones((1,), jnp.int32), (tile[1:] != tile[:-1]).astype(jnp.int32)])
    return tuple(a.astype(jnp.int32) for a in (tile, exp, lo, hi, first))


def _moe(h, w_router, b_router, w_gate_up, b_gate_up, w_down, b_down, tile_m=MOE_TILE_M):
    t = h.shape[0]
    n_pairs = t * TOP_K
    n_tiles = -(-n_pairs // tile_m)
    top_e, top_p, tile_counts, h_bf16 = _route(h, w_router, b_router)
    counts = jnp.sum(tile_counts[:, 0, :], axis=0)
    starts = jnp.concatenate([jnp.zeros((1,), jnp.int32), jnp.cumsum(counts)]).astype(jnp.int32)
    tile_base = (jnp.cumsum(tile_counts, axis=0) - tile_counts + starts[None, None, :-1]).astype(jnp.int32)
    pos = _pair_pos(top_e, tile_base)
    pair_id = jnp.arange(n_pairs, dtype=jnp.int32)
    _, order = lax.sort((top_e.reshape(-1), pair_id), num_keys=1, is_stable=True)
    row_tok = jnp.concatenate([order // TOP_K, jnp.zeros((n_tiles * tile_m - n_pairs,), jnp.int32)])
    tall = jnp.concatenate([h_bf16, jnp.zeros((n_tiles * tile_m - t + 8, h.shape[1]), bf16)], axis=0)
    tall, idx = lax.optimization_barrier((tall, row_tok))
    xs = lax.optimization_barrier(tall[idx])
    ys = _moe_experts(xs, _moe_work_items(starts, n_tiles, tile_m), w_gate_up, b_gate_up, w_down, b_down, tile_m)
    picked = [lax.optimization_barrier(ys[pos[:, j]]) for j in range(TOP_K)]
    return picked, top_p


RW_LANE_HALF = LANES // 2
RW_KJ = RW_HEAD_DIM // 2
RW_VG = RW_HEAD_DIM // 8
RW_STEPS_PER_BLOCK = 32


def _rwkv_scan_kernel(kk_ref, d_ref, kka_ref, km_ref, r_ref, v_ref, s0_ref, y_ref, s_ref, *, steps):
    tb = pl.program_id(1)

    @pl.when(tb == 0)
    def _():
        s_ref[...] = s0_ref[...]

    def fold(x):
        return x + pltpu.roll(x, RW_LANE_HALF, axis=1)

    def step(t, carry):
        acc = [None] * RW_VG
        for j in range(RW_KJ):
            kkj = kk_ref[t, pl.ds(j, 1), :]
            for g in range(RW_VG):
                p = s_ref[g, j] * kkj
                acc[g] = p if acc[g] is None else acc[g] + p
        sa = [-fold(a) for a in acc]
        vv = [v_ref[t, g] for g in range(RW_VG)]
        yacc = [None] * RW_VG
        for j in range(RW_KJ):
            dj = d_ref[t, pl.ds(j, 1), :]
            kkaj = kka_ref[t, pl.ds(j, 1), :]
            kmj = km_ref[t, pl.ds(j, 1), :]
            rj = r_ref[t, pl.ds(j, 1), :]
            for g in range(RW_VG):
                s = s_ref[g, j] * dj + sa[g] * kkaj + vv[g] * kmj
                s_ref[g, j] = s
                p = s * rj
                yacc[g] = p if yacc[g] is None else yacc[g] + p
        for g in range(RW_VG):
            y_ref[t, g] = fold(yacc[g])
        return carry

    lax.fori_loop(0, steps, step, 0)


def _rwkv_scan(kk, d, kka, km, r, v, s0):
    n_g, s_len = kk.shape[:2]
    steps = min(RW_STEPS_PER_BLOCK, s_len)
    assert s_len % steps == 0
    krow = pl.BlockSpec((None, steps, RW_KJ, LANES), lambda g, t: (g, t, 0, 0))
    vrow = pl.BlockSpec((None, steps, RW_VG, 8, LANES), lambda g, t: (g, t, 0, 0, 0))
    st = pl.BlockSpec((None, RW_VG, RW_KJ, 8, LANES), lambda g, t: (g, 0, 0, 0, 0))
    return pl.pallas_call(
        functools.partial(_rwkv_scan_kernel, steps=steps),
        grid=(n_g, s_len // steps),
        in_specs=[krow, krow, krow, krow, krow, vrow, st],
        out_specs=[vrow, st],
        out_shape=[jax.ShapeDtypeStruct(v.shape, f32), jax.ShapeDtypeStruct(s0.shape, f32)],
        compiler_params=pltpu.CompilerParams(
            dimension_semantics=("arbitrary", "arbitrary"), vmem_limit_bytes=VMEM_LIMIT_BYTES),
        name="rwkv_scan",
    )(kk, d, kka, km, r, v, s0)


def _rw_groups(n_bh):
    assert n_bh % RW_LANE_HALF == 0
    return n_bh // RW_LANE_HALF


def _rw_k_layout(x):
    b, s, h, _ = x.shape
    g = _rw_groups(b * h)
    x = x.reshape(b, s, h, 2, RW_KJ)
    x = jnp.transpose(x, (1, 4, 3, 0, 2)).reshape(s, RW_KJ, 2, g, RW_LANE_HALF)
    return jnp.transpose(x, (3, 0, 1, 2, 4)).reshape(g, s, RW_KJ, LANES)


def _rw_v_layout(x):
    b, s, h, _ = x.shape
    g = _rw_groups(b * h)
    x = jnp.transpose(x, (1, 3, 0, 2)).reshape(s, RW_HEAD_DIM, g, RW_LANE_HALF)
    x = jnp.transpose(x, (2, 0, 1, 3))
    x = jnp.concatenate([x, x], axis=-1)
    return x.reshape(g, s, RW_VG, 8, LANES)


def _rw_v_unlayout(y, b, h):
    g, s = y.shape[:2]
    y = y.reshape(g, s, RW_HEAD_DIM, LANES)[..., :RW_LANE_HALF]
    y = jnp.transpose(y, (1, 2, 0, 3)).reshape(s, RW_HEAD_DIM, b, h)
    return jnp.transpose(y, (2, 0, 3, 1))


def _rw_state_layout(wkv):
    b, h = wkv.shape[:2]
    g = _rw_groups(b * h)
    x = wkv.reshape(g, RW_LANE_HALF, RW_VG, 8, 2, RW_KJ)
    return jnp.transpose(x, (0, 2, 5, 3, 4, 1)).reshape(g, RW_VG, RW_KJ, 8, LANES)


def _rw_state_unlayout(st, b, h):
    g = st.shape[0]
    x = st.reshape(g, RW_VG, RW_KJ, 8, 2, RW_LANE_HALF)
    x = jnp.transpose(x, (0, 5, 1, 3, 4, 2))
    return x.reshape(b, h, RW_HEAD_DIM, RW_HEAD_DIM)


MIX_TILE_M = 256


def _layer_norm_rows(pre, g, b):
    mu = jnp.mean(pre, axis=1, keepdims=True)
    cen = pre - mu
    var = jnp.mean(cen * cen, axis=1, keepdims=True)
    return cen * lax.rsqrt(var + LN_EPS) * g + b


def _sigmoid(x):
    return 1.0 / (1.0 + jnp.exp(-x))


def _mix_kernel(x_ref, osb_ref, qmem_ref, orw_ref, gsb_ref, gmem_ref, grw_ref, mk_ref, mv_ref,
                wsb_ref, wmem_ref, wrw_ref, wout_ref, g1_ref, b1_ref, h_ref):
    q = qmem_ref[...].astype(bf16)
    heads = []
    for hd in range(MEM_HEADS):
        lanes = slice(hd * MEM_HEAD_DIM, (hd + 1) * MEM_HEAD_DIM)
        s = lax.dot_general(q[:, lanes], mk_ref[:, lanes].astype(bf16), (((1,), (1,)), ((), ())),
                            preferred_element_type=f32) * MEM_SCALE
        p = jnp.exp(s - jnp.max(s, axis=1, keepdims=True))
        p = p / jnp.sum(p, axis=1, keepdims=True)
        heads.append(jnp.dot(p.astype(bf16), mv_ref[:, lanes].astype(bf16), preferred_element_type=f32))
    o_mem = jnp.concatenate(heads, axis=1)
    branch = lambda o, w_ref: jnp.dot(o.astype(bf16), w_ref[...], preferred_element_type=f32)
    merged = (_sigmoid(gsb_ref[...]) * branch(osb_ref[...], wsb_ref)
              + _sigmoid(gmem_ref[...]) * branch(o_mem, wmem_ref)
              + _sigmoid(grw_ref[...]) * branch(orw_ref[...], wrw_ref))
    pre = DN_ALPHA * x_ref[...] + branch(merged, wout_ref)
    h_ref[...] = _layer_norm_rows(pre, g1_ref[...], b1_ref[...])


def _mix(x, o_sb, q_mem, o_rw, g_sb, g_mem, g_rw, mem_k, mem_v, w_sb_o, w_mem_o, w_rw_o, w_out, ln_g, ln_b):
    b, s, d = x.shape
    tile = min(MIX_TILE_M, s)
    assert s % tile == 0
    tok = lambda w: pl.BlockSpec((None, tile, w), lambda bb, i: (bb, i, 0))
    mem = pl.BlockSpec((None, N_MEM, MEM_WIDTH), lambda bb, i: (bb, 0, 0))
    const = lambda a: pl.BlockSpec(a.shape, lambda bb, i: (0,) * a.ndim)
    weights = [w.astype(bf16) for w in (w_sb_o, w_mem_o, w_rw_o, w_out)] + [ln_g[None, :], ln_b[None, :]]
    return pl.pallas_call(
        _mix_kernel,
        grid=(b, s // tile),
        in_specs=[tok(d), tok(SB_WIDTH), tok(MEM_WIDTH), tok(RW_WIDTH), tok(d), tok(d), tok(d), mem, mem]
        + [const(w) for w in weights],
        out_specs=tok(d),
        out_shape=jax.ShapeDtypeStruct((b, s, d), f32),
        compiler_params=pltpu.CompilerParams(
            dimension_semantics=("arbitrary", "arbitrary"), vmem_limit_bytes=VMEM_LIMIT_BYTES),
        name="mix_ln1",
    )(x, o_sb, q_mem, o_rw, g_sb, g_mem, g_rw, mem_k, mem_v, *weights)


def _combine_kernel(*refs):
    pick_refs = refs[:TOP_K]
    p_ref, h_ref, g_ref, b_ref, y_ref = refs[TOP_K:]
    p = p_ref[...]
    moe = pick_refs[0][...].astype(f32) * p[:, 0:1]
    for j in range(1, TOP_K):
        moe = moe + pick_refs[j][...].astype(f32) * p[:, j:j + 1]
    y_ref[...] = _layer_norm_rows(DN_ALPHA * h_ref[...] + moe, g_ref[...], b_ref[...])


def _combine(picked, top_p, h, ln_g, ln_b):
    t, d = h.shape
    tile = _token_tile(t)
    row = lambda w: pl.BlockSpec((tile, w), lambda i: (i, 0))
    vec = pl.BlockSpec((1, d), lambda i: (0, 0))
    return pl.pallas_call(
        _combine_kernel,
        grid=(t // tile,),
        in_specs=[row(d)] * TOP_K + [row(TOP_K), row(d), vec, vec],
        out_specs=row(d),
        out_shape=jax.ShapeDtypeStruct((t, d), f32),
        compiler_params=pltpu.CompilerParams(
            dimension_semantics=("arbitrary",), vmem_limit_bytes=VMEM_LIMIT_BYTES),
        name="combine_ln2",
    )(*picked, top_p, h, ln_g[None, :], ln_b[None, :])


def _sb_logits(q, k, bias):
    z = jnp.einsum('bqhd,bkhd->bhqk', q, k, preferred_element_type=f32) * SB_SCALE
    return z + bias.astype(f32)[None, :, None, None]


def _sb_weights(z, causal):
    log_keep = jnp.where(causal, jax.nn.log_sigmoid(-z), 0.0)
    log_survive = lax.cumsum(log_keep, axis=z.ndim - 1, reverse=True) - log_keep
    return jnp.where(causal, jnp.exp(jax.nn.log_sigmoid(z) + log_survive), 0.0)


def _sb_sample(q, k_new, v_new, bias, k_pool, v_pool, page_table):
    b, s = q.shape[:2]
    assert s == 1
    to_token_minor = lambda pool: jnp.transpose(pool, (0, 2, 3, 1))
    o_past = _sb_sample_past(q[:, 0], bias, to_token_minor(k_pool), to_token_minor(v_pool), page_table)
    new_pos = jnp.arange(s)
    w_new = _sb_weights(_sb_logits(q, k_new, bias), new_pos[None, :] < new_pos[:, None])
    o_new = jnp.einsum('bhqk,bkhd->bqhd', w_new, v_new)
    return (o_past[:, None] + o_new).reshape(b, s, SB_WIDTH)


def _rwkv7(p_rw, shift0, wkv0, mu_rw, w_decay0, w_decay2, w_aaa0, w_aaa2, w_gate2,
           rw_k_k, rw_k_a, rw_r_k, rw_gn_g, rw_gn_b):
    b, s, _ = p_rw.shape
    prev = jnp.concatenate([shift0[:, None, :], p_rw[:, :-1]], axis=1)
    xs = p_rw + mu_rw * (prev - p_rw)
    r, k, v, w_lo, a_lo, g_lo = _split(xs, RW_SPLITS)
    w_raw = w_decay0 + jnp.tanh(w_lo) @ w_decay2
    decay = jnp.exp(-jnp.exp(-jax.nn.softplus(-w_raw) - 0.5))
    a = jax.nn.sigmoid(w_aaa0 + a_lo @ w_aaa2)
    g = jax.nn.sigmoid(g_lo) @ w_gate2
    heads = lambda t: t.reshape(b, s, RW_HEADS, RW_HEAD_DIM)
    kk = heads(k * rw_k_k)
    kk = kk / jnp.maximum(jnp.sqrt(jnp.sum(kk * kk, axis=-1, keepdims=True)), 1e-12)
    k_mod = k * (1.0 + (a - 1.0) * rw_k_a)
    rh, kh, vh, ah, dh = heads(r), heads(k_mod), heads(v), heads(a), heads(decay)

    y, wkv = _rwkv_scan(_rw_k_layout(kk), _rw_k_layout(dh), _rw_k_layout(kk * ah), _rw_k_layout(kh),
                        _rw_k_layout(rh), _rw_v_layout(vh), _rw_state_layout(wkv0))
    y = _rw_v_unlayout(y, b, RW_HEADS)
    wkv = _rw_state_unlayout(wkv, b, RW_HEADS)
    mu = y.mean(-1, keepdims=True)
    var = jnp.square(y - mu).mean(-1, keepdims=True)
    y = ((y - mu) * lax.rsqrt(var + RW_GN_EPS)).reshape(b, s, RW_WIDTH) * rw_gn_g + rw_gn_b
    bonus = jnp.sum(rh * kh * rw_r_k, axis=-1, keepdims=True) * vh
    y = (y + bonus.reshape(b, s, RW_WIDTH)) * g
    return y, p_rw[:, -1], wkv


def _branch_mix(x, parts, o_sb, mem_k, mem_v, shift0, wkv0, lw):
    q_mem, p_rw, g_sb, g_mem, g_rw = parts
    o_rw, shift, wkv = _rwkv7(p_rw, shift0, wkv0, lw['mu_rw'], lw['w_decay0'], lw['w_decay2'],
                              lw['w_aaa0'], lw['w_aaa2'], lw['w_gate2'], lw['rw_k_k'], lw['rw_k_a'],
                              lw['rw_r_k'], lw['rw_gn_g'], lw['rw_gn_b'])
    h = _mix(x, o_sb, q_mem, o_rw, g_sb, g_mem, g_rw, mem_k, mem_v, lw['w_sb_o'], lw['w_mem_o'],
             lw['w_rw_o'], lw['w_out'], lw['ln1_g'], lw['ln1_b'])
    return h, shift, wkv


def kernel(x_prompt, x_sample, cache_sb_k, cache_sb_v, cache_mem_k, cache_mem_v, state_rw_shift,
           state_rw_wkv, page_table, mem_prompt, w_in, sb_bias, mu_rw, w_decay0, w_decay2, w_aaa0,
           w_aaa2, w_gate2, rw_k_k, rw_k_a, rw_r_k, rw_gn_g, rw_gn_b, w_mem_kv, w_sb_o, w_mem_o,
           w_rw_o, w_out, ln1_g, ln1_b, w_router, b_router, w_gate_up, b_gate_up, w_down, b_down,
           ln2_g, ln2_b):
    assert w_in.shape[0] == DEPTH == 1
    l = 0
    lw = dict(mu_rw=mu_rw[l], w_decay0=w_decay0[l], w_decay2=w_decay2[l], w_aaa0=w_aaa0[l],
              w_aaa2=w_aaa2[l], w_gate2=w_gate2[l], rw_k_k=rw_k_k[l], rw_k_a=rw_k_a[l],
              rw_r_k=rw_r_k[l], rw_gn_g=rw_gn_g[l], rw_gn_b=rw_gn_b[l], w_sb_o=w_sb_o[l],
              w_mem_o=w_mem_o[l], w_rw_o=w_rw_o[l], w_out=w_out[l], ln1_g=ln1_g[l], ln1_b=ln1_b[l])
    bp, sp, _ = x_prompt.shape
    bs, ss, _ = x_sample.shape
    w_in_b = w_in[l].astype(bf16)

    xp = x_prompt.reshape(bp * sp, D_MODEL)
    q_sb, k_sb, v_sb, *rest = _proj(xp, w_in_b, IN_SPLITS, PROJ_TILE_M)
    tok3 = lambda a: a.reshape(bp, sp, a.shape[-1])
    q_sb, k_sb, v_sb = tok3(q_sb), tok3(k_sb), tok3(v_sb)
    o_sb = _sb_prompt(q_sb, k_sb, v_sb, sb_bias[l])
    mem_k, mem_v = _proj(mem_prompt.reshape(bp * N_MEM, D_MODEL), w_mem_kv[l].astype(bf16),
                         (MEM_WIDTH, MEM_WIDTH), PROJ_TILE_M)
    mem_k = mem_k.reshape(bp, N_MEM, MEM_WIDTH)
    mem_v = mem_v.reshape(bp, N_MEM, MEM_WIDTH)
    h_p, shift_p, wkv_p = _branch_mix(
        x_prompt, [tok3(a) for a in rest], o_sb, mem_k, mem_v, jnp.zeros((bp, RW_SHIFT_WIDTH), f32),
        jnp.zeros((bp, RW_HEADS, RW_HEAD_DIM, RW_HEAD_DIM), f32), lw)

    xs = x_sample.reshape(bs * ss, D_MODEL)
    q_s, k_s, v_s, *rest_s = _proj(xs, w_in_b, IN_SPLITS, bs * ss)
    tok3s = lambda a: a.reshape(bs, ss, a.shape[-1])
    heads_s = lambda a: a.reshape(bs, ss, SB_HEADS, SB_HEAD_DIM)
    o_sb_s = _sb_sample(heads_s(q_s), heads_s(k_s), heads_s(v_s), sb_bias[l], cache_sb_k[l], cache_sb_v[l],
                        page_table)
    h_s, shift_s, wkv_s = _branch_mix(
        x_sample, [tok3s(a) for a in rest_s], o_sb_s, cache_mem_k[l].reshape(bs, N_MEM, MEM_WIDTH),
        cache_mem_v[l].reshape(bs, N_MEM, MEM_WIDTH), state_rw_shift[l], state_rw_wkv[l], lw)

    h_all = jnp.concatenate([h_p.reshape(bp * sp, D_MODEL), h_s.reshape(bs * ss, D_MODEL)], axis=0)
    picked, top_p = _moe(h_all, w_router[l], b_router[l], w_gate_up[l], b_gate_up[l], w_down[l], b_down[l])
    y_all = _combine(picked, top_p, h_all, ln2_g[l], ln2_b[l])
    y_prompt = y_all[:bp * sp].reshape(bp, sp, D_MODEL)
    y_sample = y_all[bp * sp:].reshape(bs, ss, D_MODEL)

    heads_m = lambda a: a.reshape(1, bp, N_MEM, MEM_HEADS, MEM_HEAD_DIM)
    return (y_prompt, y_sample,
            k_sb.reshape(1, bp, sp, SB_HEADS, SB_HEAD_DIM), v_sb.reshape(1, bp, sp, SB_HEADS, SB_HEAD_DIM),
            heads_m(mem_k), heads_m(mem_v), shift_p[None], wkv_p[None],
            k_s.reshape(1, bs, ss, SB_HEADS, SB_HEAD_DIM), v_s.reshape(1, bs, ss, SB_HEADS, SB_HEAD_DIM),
            shift_s[None], wkv_s[None])
```

```python
import functools

import jax
import jax.numpy as jnp
import numpy as np
from jax import lax
from jax.experimental import pallas as pl
from jax.experimental.pallas import tpu as pltpu

D_MODEL = 1024
PAGE_SIZE = 128
N_MEM = 256
SB_HEADS = 8
SB_HEAD_DIM = 64
SB_WIDTH = SB_HEADS * SB_HEAD_DIM
SB_SCALE = SB_HEAD_DIM ** -0.5
LOG2_E = 1.4426950408889634
MEM_HEADS = 4
MEM_HEAD_DIM = 128
MEM_WIDTH = MEM_HEADS * MEM_HEAD_DIM
MEM_SCALE = MEM_HEAD_DIM ** -0.5
RW_HEADS = 8
RW_HEAD_DIM = 64
RW_WIDTH = RW_HEADS * RW_HEAD_DIM
DECAY_LORA = 64
AAA_LORA = 64
GATE_LORA = 128
RW_SPLITS = (RW_WIDTH, RW_WIDTH, RW_WIDTH, DECAY_LORA, AAA_LORA, GATE_LORA)
RW_SHIFT_WIDTH = sum(RW_SPLITS)
RW_GN_EPS = 64e-5
N_EXPERTS = 32
TOP_K = 4
D_FF = D_MODEL
SWIGLU_ALPHA = 1.702
SWIGLU_LIMIT = 7.0
DEPTH = 1
DN_ALPHA = (2 * DEPTH) ** 0.25
LN_EPS = 1e-5
IN_SPLITS = (SB_WIDTH, SB_WIDTH, SB_WIDTH, MEM_WIDTH, RW_SHIFT_WIDTH, D_MODEL, D_MODEL, D_MODEL)
IN_WIDTH = sum(IN_SPLITS)

LANES = 128
MXU_DIM = 256
VMEM_LIMIT_BYTES = 48 * 1024 * 1024

SB_TILE = MXU_DIM
SB_STEP_WIDTH = 4 * SB_HEAD_DIM
PROJ_TILE_M = 256
MOE_TILE_M = 256

f32 = jnp.float32
bf16 = jnp.bfloat16


def _split(x, sizes):
    return jnp.split(x, np.cumsum(sizes)[:-1].tolist(), axis=-1)


def _proj_kernel(x_ref, w_ref, *o_refs, splits):
    xb = x_ref[...].astype(bf16)
    off = 0
    for o_ref, n in zip(o_refs, splits):
        o_ref[...] = jnp.dot(xb, w_ref[:, off:off + n], preferred_element_type=f32)
        off += n


def _proj(x, w_bf16, splits, tile_m):
    t, d = x.shape
    n = w_bf16.shape[1]
    assert t % tile_m == 0 and sum(splits) == n
    return pl.pallas_call(
        functools.partial(_proj_kernel, splits=splits),
        grid=(t // tile_m,),
        in_specs=[
            pl.BlockSpec((tile_m, d), lambda i: (i, 0)),
            pl.BlockSpec((d, n), lambda i: (0, 0), pipeline_mode=pl.Buffered(1)),
        ],
        out_specs=[pl.BlockSpec((tile_m, s), lambda i: (i, 0)) for s in splits],
        out_shape=[jax.ShapeDtypeStruct((t, s), f32) for s in splits],
        compiler_params=pltpu.CompilerParams(
            dimension_semantics=("arbitrary",), vmem_limit_bytes=VMEM_LIMIT_BYTES),
        name="proj_in",
    )(x, w_bf16)


def _sb_prompt_kernel(bias_ref, q_ref, k_ref, v_ref, o_ref, *, tile):
    hp = pl.program_id(1)
    qi = pl.program_id(2)
    heads = q_ref.shape[-1] // SB_HEAD_DIM
    row = lax.broadcasted_iota(jnp.int32, (tile, tile), 0)
    col = lax.broadcasted_iota(jnp.int32, (tile, tile), 1)
    later = (row > col).astype(bf16)
    causal = col < row
    q_all = q_ref[...] * (SB_SCALE * LOG2_E)
    qh = [q_all[:, hh * SB_HEAD_DIM:(hh + 1) * SB_HEAD_DIM].astype(bf16) for hh in range(heads)]
    bias = [bias_ref[hp * heads + hh] * LOG2_E for hh in range(heads)]

    def key_tile(j, state, masked):
        start = pl.multiple_of(j * tile, tile)
        hs = range(heads)
        lanes = [slice(h * SB_HEAD_DIM, (h + 1) * SB_HEAD_DIM) for h in hs]
        kh = [k_ref[pl.ds(start, tile), lanes[h]].astype(bf16) for h in hs]
        z2 = [lax.dot_general(qh[h], kh[h], (((1,), (1,)), ((), ())), preferred_element_type=f32) + bias[h]
              for h in hs]
        sp2 = [jnp.maximum(z2[h], 0.0) + jnp.log2(1.0 + jnp.exp2(-jnp.abs(z2[h]))) for h in hs]
        drop = [jnp.where(causal, sp2[h], 0.0) if masked else sp2[h] for h in hs]
        inner = [jnp.dot(drop[h].astype(bf16), later, preferred_element_type=f32) for h in hs]
        a = [jnp.exp2(z2[h] - (sp2[h] + state[h][0] + inner[h])) for h in hs]
        if masked:
            a = [jnp.where(causal, a[h], 0.0) for h in hs]
        vh = [v_ref[pl.ds(start, tile), lanes[h]].astype(bf16) for h in hs]
        acc = [state[h][1] + jnp.dot(a[h].astype(bf16), vh[h], preferred_element_type=f32) for h in hs]
        carry = [state[h][0] + jnp.sum(drop[h], axis=1, keepdims=True) for h in hs]
        return tuple((carry[h], acc[h]) for h in hs)

    state = tuple((jnp.zeros((tile, 1), f32), jnp.zeros((tile, SB_HEAD_DIM), f32)) for _ in range(heads))
    state = key_tile(qi, state, True)
    state = lax.fori_loop(0, qi, lambda s, st: key_tile(qi - 1 - s, st, False), state)
    o_ref[...] = jnp.concatenate([acc for _, acc in state], axis=1)


def _sb_prompt(q, k, v, bias, tile=SB_TILE):
    b, s, w = q.shape
    assert s % tile == 0 and w % SB_STEP_WIDTH == 0
    blk = lambda bb, hp, qi: (bb, qi, hp)
    full = lambda bb, hp, qi: (bb, 0, hp)
    return pl.pallas_call(
        functools.partial(_sb_prompt_kernel, tile=tile),
        grid=(b, w // SB_STEP_WIDTH, s // tile),
        in_specs=[
            pl.BlockSpec(memory_space=pltpu.SMEM),
            pl.BlockSpec((None, tile, SB_STEP_WIDTH), blk),
            pl.BlockSpec((None, s, SB_STEP_WIDTH), full),
            pl.BlockSpec((None, s, SB_STEP_WIDTH), full),
        ],
        out_specs=pl.BlockSpec((None, tile, SB_STEP_WIDTH), blk),
        out_shape=jax.ShapeDtypeStruct((b, s, w), f32),
        compiler_params=pltpu.CompilerParams(
            dimension_semantics=("arbitrary", "arbitrary", "arbitrary"),
            vmem_limit_bytes=VMEM_LIMIT_BYTES),
        name="sb_prompt",
    )(bias, q, k, v)


SBS_PAGES_PER_STEP = 16


def _sb_sample_kernel(pt_ref, bias_ref, q_ref, *refs, n_slots):
    del pt_ref
    k_refs = refs[:n_slots]
    v_refs = refs[n_slots:2 * n_slots]
    o_ref = refs[2 * n_slots]
    qb_ref, acc_ref, carry_ref, z_ref = refs[2 * n_slots + 1:]
    step = pl.program_id(1)

    @pl.when(step == 0)
    def _():
        q = q_ref[...] * SB_SCALE
        qb_ref[...] = jnp.broadcast_to(q, qb_ref.shape)
        acc_ref[...] = jnp.zeros_like(acc_ref)
        carry_ref[...] = jnp.zeros_like(carry_ref)

    row = lax.broadcasted_iota(jnp.int32, (PAGE_SIZE, PAGE_SIZE), 0)
    col = lax.broadcasted_iota(jnp.int32, (PAGE_SIZE, PAGE_SIZE), 1)
    later = (row > col).astype(bf16)
    later2 = jnp.concatenate([later, later], axis=0)

    for i, k_ref in enumerate(k_refs):
        for h in range(SB_HEADS):
            zh = jnp.sum(k_ref[h] * qb_ref[h], axis=0, keepdims=True) + bias_ref[h]
            z_ref[pl.ds(i * SB_HEADS + h, 1), :] = zh
    z = z_ref[...]
    softplus = jnp.maximum(z, 0.0) + jnp.log(1.0 + jnp.exp(-jnp.abs(z)))
    log_keep = -softplus
    hi = log_keep.astype(bf16)
    lo = (log_keep - hi.astype(f32)).astype(bf16)
    inner = jnp.dot(jnp.concatenate([hi, lo], axis=1), later2, preferred_element_type=f32)
    total = jnp.sum(log_keep, axis=1, keepdims=True)
    base = z - softplus + inner
    carry = carry_ref[...]
    for i in range(n_slots):
        sl = slice(i * SB_HEADS, (i + 1) * SB_HEADS)
        w = jnp.exp(base[sl] + carry)
        for h in range(SB_HEADS):
            acc_ref[h] += v_refs[i][h] * w[h:h + 1, :]
        carry = carry + total[sl]
    carry_ref[...] = carry

    @pl.when(step == pl.num_programs(1) - 1)
    def _():
        o_ref[...] = jnp.sum(acc_ref[...], axis=2, keepdims=True)


def _sb_sample_past(q, bias, k_pool, v_pool, page_table):
    b, n_pages = page_table.shape
    n_slots = SBS_PAGES_PER_STEP
    assert n_pages % n_slots == 0
    page_block = (None, SB_HEADS, SB_HEAD_DIM, PAGE_SIZE)

    def page_spec(slot):
        return pl.BlockSpec(
            page_block, lambda bb, s, pt, slot=slot: (pt[bb, n_pages - 1 - (s * n_slots + slot)], 0, 0, 0))

    grid_spec = pltpu.PrefetchScalarGridSpec(
        num_scalar_prefetch=1,
        grid=(b, n_pages // n_slots),
        in_specs=[
            pl.BlockSpec(memory_space=pltpu.SMEM),
            pl.BlockSpec((None, SB_HEADS, SB_HEAD_DIM, 1), lambda bb, s, pt: (bb, 0, 0, 0)),
        ] + [page_spec(i) for i in range(n_slots)] * 2,
        out_specs=pl.BlockSpec((None, SB_HEADS, SB_HEAD_DIM, 1), lambda bb, s, pt: (bb, 0, 0, 0)),
        scratch_shapes=[
            pltpu.VMEM((SB_HEADS, SB_HEAD_DIM, PAGE_SIZE), f32),
            pltpu.VMEM((SB_HEADS, SB_HEAD_DIM, PAGE_SIZE), f32),
            pltpu.VMEM((SB_HEADS, 1), f32),
            pltpu.VMEM((n_slots * SB_HEADS, PAGE_SIZE), f32),
        ],
    )
    o = pl.pallas_call(
        functools.partial(_sb_sample_kernel, n_slots=n_slots),
        grid_spec=grid_spec,
        out_shape=jax.ShapeDtypeStruct((b, SB_HEADS, SB_HEAD_DIM, 1), f32),
        compiler_params=pltpu.CompilerParams(
            dimension_semantics=("arbitrary", "arbitrary"), vmem_limit_bytes=VMEM_LIMIT_BYTES),
        name="sb_sample",
    )(page_table, bias, q[..., None], *([k_pool] * n_slots), *([v_pool] * n_slots))
    return o[..., 0]


def _token_tile(t):
    return max(m for m in range(8, 513, 8) if t % m == 0)


def _route_kernel(h_ref, whi_ref, wlo_ref, b_ref, e_ref, p_ref, cnt_ref, hb_ref):
    h = h_ref[...]
    h_hi = h.astype(bf16)
    hb_ref[...] = h_hi
    h_lo = (h - h_hi.astype(f32)).astype(bf16)
    dot = lambda a, w_ref: jnp.dot(a, w_ref[...], preferred_element_type=f32)
    work = dot(h_hi, whi_ref) + (dot(h_hi, wlo_ref) + dot(h_lo, whi_ref)) + b_ref[...]
    lane = lax.broadcasted_iota(jnp.int32, work.shape, 1)
    picked = jnp.zeros(work.shape, jnp.bool_)
    top_e, top_l = [], []
    for _ in range(TOP_K):
        m = jnp.max(work, axis=1, keepdims=True)
        idx = jnp.min(jnp.where(work == m, lane, N_EXPERTS), axis=1, keepdims=True)
        hit = lane == idx
        picked = picked | hit
        work = jnp.where(hit, -jnp.inf, work)
        top_e.append(idx)
        top_l.append(m)
    ex = [jnp.exp(l - top_l[0]) for l in top_l]
    den = ex[0] + ex[1] + ex[2] + ex[3]
    e_ref[...] = jnp.concatenate(top_e, axis=1)
    p_ref[...] = jnp.concatenate([x / den for x in ex], axis=1)
    cnt_ref[...] = jnp.sum(picked.astype(f32), axis=0, keepdims=True).astype(jnp.int32)


def _route(h, w_router, b_router):
    t, d = h.shape
    tile = _token_tile(t)
    w_hi = w_router.astype(bf16)
    w_lo = (w_router - w_hi.astype(f32)).astype(bf16)
    const = lambda a: pl.BlockSpec(a.shape, lambda i: (0,) * a.ndim)
    b2 = b_router[None, :]
    return pl.pallas_call(
        _route_kernel,
        grid=(t // tile,),
        in_specs=[pl.BlockSpec((tile, d), lambda i: (i, 0)), const(w_hi), const(w_lo), const(b2)],
        out_specs=[pl.BlockSpec((tile, TOP_K), lambda i: (i, 0)), pl.BlockSpec((tile, TOP_K), lambda i: (i, 0)),
                   pl.BlockSpec((None, 1, N_EXPERTS), lambda i: (i, 0, 0)), pl.BlockSpec((tile, d), lambda i: (i, 0))],
        out_shape=[jax.ShapeDtypeStruct((t, TOP_K), jnp.int32), jax.ShapeDtypeStruct((t, TOP_K), f32),
                   jax.ShapeDtypeStruct((t // tile, 1, N_EXPERTS), jnp.int32), jax.ShapeDtypeStruct((t, d), bf16)],
        compiler_params=pltpu.CompilerParams(
            dimension_semantics=("arbitrary",), vmem_limit_bytes=VMEM_LIMIT_BYTES),
        name="moe_route",
    )(h, w_hi, w_lo, b2)


def _pair_pos_kernel(e_ref, base_ref, pos_ref):
    tile = e_ref.shape[0]
    lane = lax.broadcasted_iota(jnp.int32, (tile, N_EXPERTS), 1)
    hits = [lane == e_ref[:, j:j + 1] for j in range(TOP_K)]
    any_hit = hits[0] | hits[1] | hits[2] | hits[3]
    row = lax.broadcasted_iota(jnp.int32, (tile, tile), 0)
    col = lax.broadcasted_iota(jnp.int32, (tile, tile), 1)
    earlier = (col < row).astype(bf16)
    before = jnp.dot(earlier, any_hit.astype(bf16), preferred_element_type=f32) + base_ref[...].astype(f32)
    pos = [jnp.sum(jnp.where(hit, before, 0.0), axis=1, keepdims=True) for hit in hits]
    pos_ref[...] = jnp.concatenate(pos, axis=1).astype(jnp.int32)


def _pair_pos(top_e, tile_base):
    t = top_e.shape[0]
    n_tiles = tile_base.shape[0]
    tile = t // n_tiles
    return pl.pallas_call(
        _pair_pos_kernel,
        grid=(n_tiles,),
        in_specs=[pl.BlockSpec((tile, TOP_K), lambda i: (i, 0)),
                  pl.BlockSpec((None, 1, N_EXPERTS), lambda i: (i, 0, 0))],
        out_specs=pl.BlockSpec((tile, TOP_K), lambda i: (i, 0)),
        out_shape=jax.ShapeDtypeStruct((t, TOP_K), jnp.int32),
        compiler_params=pltpu.CompilerParams(
            dimension_semantics=("arbitrary",), vmem_limit_bytes=VMEM_LIMIT_BYTES),
        name="moe_pair_pos",
    )(top_e, tile_base)


def _moe_kernel(tile_ref, exp_ref, lo_ref, hi_ref, first_ref, x_ref, wgu_ref, bg_ref, bl_ref, wd_ref, bd_ref,
                o_ref, wg_s, wl_s, wd_s):
    del tile_ref
    i = pl.program_id(0)
    ff = wd_ref.shape[0]
    half = MXU_DIM // 2

    src = lax.broadcasted_iota(jnp.int32, (MXU_DIM, MXU_DIM), 0)
    dst = lax.broadcasted_iota(jnp.int32, (MXU_DIM, MXU_DIM), 1)
    unzip = (src == jnp.where(dst < half, 2 * dst, 2 * (dst - half) + 1)).astype(bf16)

    @pl.when((i == 0) | (exp_ref[i] != exp_ref[jnp.maximum(i - 1, 0)]))
    def _():
        for c in range(2 * ff // MXU_DIM):
            cols = slice(c * MXU_DIM, (c + 1) * MXU_DIM)
            out = slice(c * half, (c + 1) * half)
            w = jnp.dot(wgu_ref[:, cols].astype(bf16), unzip, preferred_element_type=f32).astype(bf16)
            wg_s[:, out] = w[:, :half]
            wl_s[:, out] = w[:, half:]
        wd_s[...] = wd_ref[...].astype(bf16)

    @pl.when(first_ref[i] == 1)
    def _():
        o_ref[...] = jnp.zeros_like(o_ref)

    @pl.when(hi_ref[i] > lo_ref[i])
    def _():
        x = x_ref[...]
        g = jnp.dot(x, wg_s[...], preferred_element_type=f32) + bg_ref[...]
        l = jnp.dot(x, wl_s[...], preferred_element_type=f32) + bl_ref[...]
        glu = jnp.minimum(g, SWIGLU_LIMIT)
        lin = jnp.clip(l, -SWIGLU_LIMIT, SWIGLU_LIMIT)
        h = glu * (1.0 / (1.0 + jnp.exp(-SWIGLU_ALPHA * glu))) * (lin + 1.0)
        y = jnp.dot(h.astype(bf16), wd_s[...], preferred_element_type=f32) + bd_ref[...]
        row = lax.broadcasted_iota(jnp.int32, (x.shape[0], 1), 0)
        mine = (row >= lo_ref[i]) & (row < hi_ref[i])
        o_ref[...] = jnp.where(mine, y.astype(o_ref.dtype), o_ref[...])


def _moe_experts(xs, items, w_gate_up, b_gate_up, w_down, b_down, tile_m):
    n_rows, d = xs.shape
    ff = w_down.shape[1]
    n_items = items[0].shape[0]
    row = lambda i, tl, ex, lo, hi, fi: (tl[i], 0)
    wsel = lambda i, tl, ex, lo, hi, fi: (ex[i], 0, 0)
    grid_spec = pltpu.PrefetchScalarGridSpec(
        num_scalar_prefetch=5,
        grid=(n_items,),
        in_specs=[
            pl.BlockSpec((tile_m, d), row),
            pl.BlockSpec((None, d, 2 * ff), wsel),
            pl.BlockSpec((None, 1, ff), wsel),
            pl.BlockSpec((None, 1, ff), wsel),
            pl.BlockSpec((None, ff, d), wsel),
            pl.BlockSpec((None, 1, d), wsel),
        ],
        out_specs=pl.BlockSpec((tile_m, d), row),
        scratch_shapes=[
            pltpu.VMEM((d, ff), bf16), pltpu.VMEM((d, ff), bf16), pltpu.VMEM((ff, d), bf16),
        ],
    )
    return pl.pallas_call(
        _moe_kernel,
        grid_spec=grid_spec,
        out_shape=jax.ShapeDtypeStruct((n_rows, d), bf16),
        compiler_params=pltpu.CompilerParams(
            dimension_semantics=("arbitrary",), vmem_limit_bytes=VMEM_LIMIT_BYTES),
        name="moe_experts",
    )(*items, xs, w_gate_up, b_gate_up[:, None, 0::2], b_gate_up[:, None, 1::2], w_down, b_down[:, None, :])


def _moe_work_items(starts, n_tiles, tile_m):
    counts = starts[1:] - starts[:-1]
    first_tile = starts[:-1] // tile_m
    n_t = jnp.where(counts > 0, (starts[1:] - 1) // tile_m - first_tile + 1, 0)
    item_end = jnp.cumsum(n_t)
    item_start = item_end - n_t
    n_items = n_tiles + N_EXPERTS - 1
    i = jnp.arange(n_items, dtype=jnp.int32)
    valid = i < item_end[-1]
    ii = jnp.minimum(i, item_end[-1] - 1)
    exp = jnp.sum((ii[:, None] >= item_end[None, :]).astype(jnp.int32), axis=1)
    onehot = (exp[:, None] == jnp.arange(N_EXPERTS, dtype=jnp.int32)[None, :]).astype(jnp.int32)
    pick = lambda table: jnp.sum(onehot * table[None, :], axis=1)
    tile = pick(first_tile) + ii - pick(item_start)
    lo = jnp.clip(pick(starts[:-1]) - tile * tile_m, 0, tile_m)
    hi = jnp.clip(pick(starts[1:]) - tile * tile_m, 0, tile_m)
    hi = jnp.where(valid, hi, lo)
    first = jnp.concatenate([jnp.ones((1,), jnp.int32), (tile[1:] != tile[:-1]).astype(jnp.int32)])
    return tuple(a.astype(jnp.int32) for a in (tile, exp, lo, hi, first))


def _moe(h, w_router, b_router, w_gate_up, b_gate_up, w_down, b_down, tile_m=MOE_TILE_M):
    t = h.shape[0]
    n_pairs = t * TOP_K
    n_tiles = -(-n_pairs // tile_m)
    top_e, top_p, tile_counts, h_bf16 = _route(h, w_router, b_router)
    counts = jnp.sum(tile_counts[:, 0, :], axis=0)
    starts = jnp.concatenate([jnp.zeros((1,), jnp.int32), jnp.cumsum(counts)]).astype(jnp.int32)
    tile_base = (jnp.cumsum(tile_counts, axis=0) - tile_counts + starts[None, None, :-1]).astype(jnp.int32)
    pos = _pair_pos(top_e, tile_base)
    pair_id = jnp.arange(n_pairs, dtype=jnp.int32)
    _, order = lax.sort((top_e.reshape(-1), pair_id), num_keys=1, is_stable=True)
    row_tok = jnp.concatenate([order // TOP_K, jnp.zeros((n_tiles * tile_m - n_pairs,), jnp.int32)])
    tall = jnp.concatenate([h_bf16, jnp.zeros((n_tiles * tile_m - t + 8, h.shape[1]), bf16)], axis=0)
    tall, idx = lax.optimization_barrier((tall, row_tok))
    xs = lax.optimization_barrier(tall[idx])
    ys = _moe_experts(xs, _moe_work_items(starts, n_tiles, tile_m), w_gate_up, b_gate_up, w_down, b_down, tile_m)
    picked = [lax.optimization_barrier(ys[pos[:, j]]) for j in range(TOP_K)]
    return picked, top_p


RW_LANE_HALF = LANES // 2
RW_KJ = RW_HEAD_DIM // 2
RW_VG = RW_HEAD_DIM // 8
RW_STEPS_PER_BLOCK = 32


def _rwkv_scan_kernel(kk_ref, d_ref, kka_ref, km_ref, r_ref, v_ref, s0_ref, y_ref, s_ref, *, steps):
    tb = pl.program_id(1)

    @pl.when(tb == 0)
    def _():
        s_ref[...] = s0_ref[...]

    def fold(x):
        return x + pltpu.roll(x, RW_LANE_HALF, axis=1)

    def step(t, carry):
        acc = [None] * RW_VG
        for j in range(RW_KJ):
            kkj = kk_ref[t, pl.ds(j, 1), :]
            for g in range(RW_VG):
                p = s_ref[g, j] * kkj
                acc[g] = p if acc[g] is None else acc[g] + p
        sa = [-fold(a) for a in acc]
        vv = [v_ref[t, g] for g in range(RW_VG)]
        yacc = [None] * RW_VG
        for j in range(RW_KJ):
            dj = d_ref[t, pl.ds(j, 1), :]
            kkaj = kka_ref[t, pl.ds(j, 1), :]
            kmj = km_ref[t, pl.ds(j, 1), :]
            rj = r_ref[t, pl.ds(j, 1), :]
            for g in range(RW_VG):
                s = s_ref[g, j] * dj + sa[g] * kkaj + vv[g] * kmj
                s_ref[g, j] = s
                p = s * rj
                yacc[g] = p if yacc[g] is None else yacc[g] + p
        for g in range(RW_VG):
            y_ref[t, g] = fold(yacc[g])
        return carry

    lax.fori_loop(0, steps, step, 0, unroll=min(2, steps))


def _rwkv_scan(kk, d, kka, km, r, v, s0):
    n_g, s_len = kk.shape[:2]
    steps = min(RW_STEPS_PER_BLOCK, s_len)
    assert s_len % steps == 0
    krow = pl.BlockSpec((None, steps, RW_KJ, LANES), lambda g, t: (g, t, 0, 0))
    vrow = pl.BlockSpec((None, steps, RW_VG, 8, LANES), lambda g, t: (g, t, 0, 0, 0))
    st = pl.BlockSpec((None, RW_VG, RW_KJ, 8, LANES), lambda g, t: (g, 0, 0, 0, 0))
    return pl.pallas_call(
        functools.partial(_rwkv_scan_kernel, steps=steps),
        grid=(n_g, s_len // steps),
        in_specs=[krow, krow, krow, krow, krow, vrow, st],
        out_specs=[vrow, st],
        out_shape=[jax.ShapeDtypeStruct(v.shape, f32), jax.ShapeDtypeStruct(s0.shape, f32)],
        compiler_params=pltpu.CompilerParams(
            dimension_semantics=("arbitrary", "arbitrary"), vmem_limit_bytes=VMEM_LIMIT_BYTES),
        name="rwkv_scan",
    )(kk, d, kka, km, r, v, s0)


def _rw_groups(n_bh):
    assert n_bh % RW_LANE_HALF == 0
    return n_bh // RW_LANE_HALF


def _rw_k_layout(x):
    b, s, h, _ = x.shape
    g = _rw_groups(b * h)
    x = x.reshape(b, s, h, 2, RW_KJ)
    x = jnp.transpose(x, (1, 4, 3, 0, 2)).reshape(s, RW_KJ, 2, g, RW_LANE_HALF)
    return jnp.transpose(x, (3, 0, 1, 2, 4)).reshape(g, s, RW_KJ, LANES)


def _rw_v_layout(x):
    b, s, h, _ = x.shape
    g = _rw_groups(b * h)
    x = jnp.transpose(x, (1, 3, 0, 2)).reshape(s, RW_HEAD_DIM, g, RW_LANE_HALF)
    x = jnp.transpose(x, (2, 0, 1, 3))
    x = jnp.concatenate([x, x], axis=-1)
    return x.reshape(g, s, RW_VG, 8, LANES)


def _rw_v_unlayout(y, b, h):
    g, s = y.shape[:2]
    y = y.reshape(g, s, RW_HEAD_DIM, LANES)[..., :RW_LANE_HALF]
    y = jnp.transpose(y, (1, 2, 0, 3)).reshape(s, RW_HEAD_DIM, b, h)
    return jnp.transpose(y, (2, 0, 3, 1))


def _rw_state_layout(wkv):
    b, h = wkv.shape[:2]
    g = _rw_groups(b * h)
    x = wkv.reshape(g, RW_LANE_HALF, RW_VG, 8, 2, RW_KJ)
    return jnp.transpose(x, (0, 2, 5, 3, 4, 1)).reshape(g, RW_VG, RW_KJ, 8, LANES)


def _rw_state_unlayout(st, b, h):
    g = st.shape[0]
    x = st.reshape(g, RW_VG, RW_KJ, 8, 2, RW_LANE_HALF)
    x = jnp.transpose(x, (0, 5, 1, 3, 4, 2))
    return x.reshape(b, h, RW_HEAD_DIM, RW_HEAD_DIM)


MIX_TILE_M = 256


def _layer_norm_rows(pre, g, b):
    mu = jnp.mean(pre, axis=1, keepdims=True)
    cen = pre - mu
    var = jnp.mean(cen * cen, axis=1, keepdims=True)
    return cen * lax.rsqrt(var + LN_EPS) * g + b


def _sigmoid(x):
    return 1.0 / (1.0 + jnp.exp(-x))


def _mem_attend(q, mk_ref, mv_ref):
    heads = []
    for hd in range(MEM_HEADS):
        lanes = slice(hd * MEM_HEAD_DIM, (hd + 1) * MEM_HEAD_DIM)
        s = lax.dot_general(q[:, lanes], mk_ref[:, lanes].astype(bf16), (((1,), (1,)), ((), ())),
                            preferred_element_type=f32) * MEM_SCALE
        p = jnp.exp(s - jnp.max(s, axis=1, keepdims=True))
        p = p / jnp.sum(p, axis=1, keepdims=True)
        heads.append(jnp.dot(p.astype(bf16), mv_ref[:, lanes].astype(bf16), preferred_element_type=f32))
    return jnp.concatenate(heads, axis=1)


def _mem_attend_kernel(q_ref, mk_ref, mv_ref, o_ref):
    o_ref[...] = _mem_attend(q_ref[...].astype(bf16), mk_ref, mv_ref)


def _mem_attend_tokens(q_mem, mem_k, mem_v):
    b, s, w = q_mem.shape
    tok = pl.BlockSpec((None, s, w), lambda bb: (bb, 0, 0))
    mem = pl.BlockSpec((None, N_MEM, MEM_WIDTH), lambda bb: (bb, 0, 0))
    return pl.pallas_call(
        _mem_attend_kernel,
        grid=(b,),
        in_specs=[tok, mem, mem],
        out_specs=tok,
        out_shape=jax.ShapeDtypeStruct((b, s, w), f32),
        compiler_params=pltpu.CompilerParams(
            dimension_semantics=("arbitrary",), vmem_limit_bytes=VMEM_LIMIT_BYTES),
        name="mem_attend",
    )(q_mem, mem_k, mem_v)


def _mix_kernel(*refs, attend):
    if attend:
        x_ref, osb_ref, third_ref, orw_ref, gsb_ref, gmem_ref, grw_ref, mk_ref, mv_ref = refs[:9]
        o_mem = _mem_attend(third_ref[...].astype(bf16), mk_ref, mv_ref)
    else:
        x_ref, osb_ref, third_ref, orw_ref, gsb_ref, gmem_ref, grw_ref = refs[:7]
        o_mem = third_ref[...]
    wsb_ref, wmem_ref, wrw_ref, wout_ref, g1_ref, b1_ref, h_ref = refs[-7:]
    branch = lambda o, w_ref: jnp.dot(o.astype(bf16), w_ref[...], preferred_element_type=f32)
    merged = (_sigmoid(gsb_ref[...]) * branch(osb_ref[...], wsb_ref)
              + _sigmoid(gmem_ref[...]) * branch(o_mem, wmem_ref)
              + _sigmoid(grw_ref[...]) * branch(orw_ref[...], wrw_ref))
    pre = DN_ALPHA * x_ref[...] + branch(merged, wout_ref)
    h_ref[...] = _layer_norm_rows(pre, g1_ref[...], b1_ref[...])


def _mix(x, o_sb, q_or_o_mem, o_rw, g_sb, g_mem, g_rw, mem_kv, w_sb_o, w_mem_o, w_rw_o, w_out, ln_g, ln_b):
    b, s, d = x.shape
    tile = min(MIX_TILE_M, s)
    assert s % tile == 0
    tok = lambda w: pl.BlockSpec((None, tile, w), lambda bb, i: (bb, i, 0))
    mem = pl.BlockSpec((None, N_MEM, MEM_WIDTH), lambda bb, i: (bb, 0, 0))
    const = lambda a: pl.BlockSpec(a.shape, lambda bb, i: (0,) * a.ndim)
    weights = [w.astype(bf16) for w in (w_sb_o, w_mem_o, w_rw_o, w_out)] + [ln_g[None, :], ln_b[None, :]]
    mems = () if mem_kv is None else tuple(mem_kv)
    return pl.pallas_call(
        functools.partial(_mix_kernel, attend=mem_kv is not None),
        grid=(b, s // tile),
        in_specs=[tok(d), tok(SB_WIDTH), tok(MEM_WIDTH), tok(RW_WIDTH), tok(d), tok(d), tok(d)]
        + [mem] * len(mems) + [const(w) for w in weights],
        out_specs=tok(d),
        out_shape=jax.ShapeDtypeStruct((b, s, d), f32),
        compiler_params=pltpu.CompilerParams(
            dimension_semantics=("arbitrary", "arbitrary"), vmem_limit_bytes=VMEM_LIMIT_BYTES),
        name="mix_ln1",
    )(x, o_sb, q_or_o_mem, o_rw, g_sb, g_mem, g_rw, *mems, *weights)


def _combine_kernel(*refs):
    pick_refs = refs[:TOP_K]
    p_ref, h_ref, g_ref, b_ref, y_ref = refs[TOP_K:]
    p = p_ref[...]
    moe = pick_refs[0][...].astype(f32) * p[:, 0:1]
    for j in range(1, TOP_K):
        moe = moe + pick_refs[j][...].astype(f32) * p[:, j:j + 1]
    y_ref[...] = _layer_norm_rows(DN_ALPHA * h_ref[...] + moe, g_ref[...], b_ref[...])


def _combine(picked, top_p, h, ln_g, ln_b, first_row, n_rows):
    d = h.shape[1]
    tile = min(MIX_TILE_M, n_rows)
    assert n_rows % tile == 0 and first_row % tile == 0
    first = first_row // tile
    row = lambda w: pl.BlockSpec((tile, w), lambda i: (first + i, 0))
    vec = pl.BlockSpec((1, d), lambda i: (0, 0))
    return pl.pallas_call(
        _combine_kernel,
        grid=(n_rows // tile,),
        in_specs=[row(d)] * TOP_K + [row(TOP_K), row(d), vec, vec],
        out_specs=pl.BlockSpec((tile, d), lambda i: (i, 0)),
        out_shape=jax.ShapeDtypeStruct((n_rows, d), f32),
        compiler_params=pltpu.CompilerParams(
            dimension_semantics=("arbitrary",), vmem_limit_bytes=VMEM_LIMIT_BYTES),
        name="combine_ln2",
    )(*picked, top_p, h, ln_g[None, :], ln_b[None, :])


def _sb_logits(q, k, bias):
    z = jnp.einsum('bqhd,bkhd->bhqk', q, k, preferred_element_type=f32) * SB_SCALE
    return z + bias.astype(f32)[None, :, None, None]


def _sb_weights(z, causal):
    log_keep = jnp.where(causal, jax.nn.log_sigmoid(-z), 0.0)
    log_survive = lax.cumsum(log_keep, axis=z.ndim - 1, reverse=True) - log_keep
    return jnp.where(causal, jnp.exp(jax.nn.log_sigmoid(z) + log_survive), 0.0)


def _sb_sample(q, k_new, v_new, bias, k_pool, v_pool, page_table):
    b, s = q.shape[:2]
    assert s == 1
    to_token_minor = lambda pool: jnp.transpose(pool, (0, 2, 3, 1))
    o_past = _sb_sample_past(q[:, 0], bias, to_token_minor(k_pool), to_token_minor(v_pool), page_table)
    new_pos = jnp.arange(s)
    w_new = _sb_weights(_sb_logits(q, k_new, bias), new_pos[None, :] < new_pos[:, None])
    o_new = jnp.einsum('bhqk,bkhd->bqhd', w_new, v_new)
    return (o_past[:, None] + o_new).reshape(b, s, SB_WIDTH)


def _rwkv7(p_rw, shift0, wkv0, mu_rw, w_decay0, w_decay2, w_aaa0, w_aaa2, w_gate2,
           rw_k_k, rw_k_a, rw_r_k, rw_gn_g, rw_gn_b):
    b, s, _ = p_rw.shape
    prev = jnp.concatenate([shift0[:, None, :], p_rw[:, :-1]], axis=1)
    xs = p_rw + mu_rw * (prev - p_rw)
    r, k, v, w_lo, a_lo, g_lo = _split(xs, RW_SPLITS)
    w_raw = w_decay0 + jnp.tanh(w_lo) @ w_decay2
    decay = jnp.exp(-jnp.exp(-jax.nn.softplus(-w_raw) - 0.5))
    a = jax.nn.sigmoid(w_aaa0 + a_lo @ w_aaa2)
    g = jax.nn.sigmoid(g_lo) @ w_gate2
    heads = lambda t: t.reshape(b, s, RW_HEADS, RW_HEAD_DIM)
    kk = heads(k * rw_k_k)
    kk = kk / jnp.maximum(jnp.sqrt(jnp.sum(kk * kk, axis=-1, keepdims=True)), 1e-12)
    k_mod = k * (1.0 + (a - 1.0) * rw_k_a)
    rh, kh, vh, ah, dh = heads(r), heads(k_mod), heads(v), heads(a), heads(decay)

    y, wkv = _rwkv_scan(_rw_k_layout(kk), _rw_k_layout(dh), _rw_k_layout(kk * ah), _rw_k_layout(kh),
                        _rw_k_layout(rh), _rw_v_layout(vh), _rw_state_layout(wkv0))
    y = _rw_v_unlayout(y, b, RW_HEADS)
    wkv = _rw_state_unlayout(wkv, b, RW_HEADS)
    mu = y.mean(-1, keepdims=True)
    var = jnp.square(y - mu).mean(-1, keepdims=True)
    y = ((y - mu) * lax.rsqrt(var + RW_GN_EPS)).reshape(b, s, RW_WIDTH) * rw_gn_g + rw_gn_b
    bonus = jnp.sum(rh * kh * rw_r_k, axis=-1, keepdims=True) * vh
    y = (y + bonus.reshape(b, s, RW_WIDTH)) * g
    return y, p_rw[:, -1], wkv


def _branch_mix(x, parts, o_sb, mem_k, mem_v, shift0, wkv0, lw):
    q_mem, p_rw, g_sb, g_mem, g_rw = parts
    o_rw, shift, wkv = _rwkv7(p_rw, shift0, wkv0, lw['mu_rw'], lw['w_decay0'], lw['w_decay2'],
                              lw['w_aaa0'], lw['w_aaa2'], lw['w_gate2'], lw['rw_k_k'], lw['rw_k_a'],
                              lw['rw_r_k'], lw['rw_gn_g'], lw['rw_gn_b'])
    tail = (lw['w_sb_o'], lw['w_mem_o'], lw['w_rw_o'], lw['w_out'], lw['ln1_g'], lw['ln1_b'])
    if x.shape[1] == 1:
        rows = lambda a: a.reshape(1, a.shape[0], a.shape[2])
        o_mem = _mem_attend_tokens(q_mem, mem_k, mem_v)
        h = _mix(rows(x), rows(o_sb), rows(o_mem), rows(o_rw), rows(g_sb), rows(g_mem), rows(g_rw), None, *tail)
        h = h.reshape(x.shape)
    else:
        h = _mix(x, o_sb, q_mem, o_rw, g_sb, g_mem, g_rw, (mem_k, mem_v), *tail)
    return h, shift, wkv


def kernel(x_prompt, x_sample, cache_sb_k, cache_sb_v, cache_mem_k, cache_mem_v, state_rw_shift,
           state_rw_wkv, page_table, mem_prompt, w_in, sb_bias, mu_rw, w_decay0, w_decay2, w_aaa0,
           w_aaa2, w_gate2, rw_k_k, rw_k_a, rw_r_k, rw_gn_g, rw_gn_b, w_mem_kv, w_sb_o, w_mem_o,
           w_rw_o, w_out, ln1_g, ln1_b, w_router, b_router, w_gate_up, b_gate_up, w_down, b_down,
           ln2_g, ln2_b):
    assert w_in.shape[0] == DEPTH == 1
    l = 0
    lw = dict(mu_rw=mu_rw[l], w_decay0=w_decay0[l], w_decay2=w_decay2[l], w_aaa0=w_aaa0[l],
              w_aaa2=w_aaa2[l], w_gate2=w_gate2[l], rw_k_k=rw_k_k[l], rw_k_a=rw_k_a[l],
              rw_r_k=rw_r_k[l], rw_gn_g=rw_gn_g[l], rw_gn_b=rw_gn_b[l], w_sb_o=w_sb_o[l],
              w_mem_o=w_mem_o[l], w_rw_o=w_rw_o[l], w_out=w_out[l], ln1_g=ln1_g[l], ln1_b=ln1_b[l])
    bp, sp, _ = x_prompt.shape
    bs, ss, _ = x_sample.shape
    w_in_b = w_in[l].astype(bf16)

    xp = x_prompt.reshape(bp * sp, D_MODEL)
    q_sb, k_sb, v_sb, *rest = _proj(xp, w_in_b, IN_SPLITS, PROJ_TILE_M)
    tok3 = lambda a: a.reshape(bp, sp, a.shape[-1])
    q_sb, k_sb, v_sb = tok3(q_sb), tok3(k_sb), tok3(v_sb)
    o_sb = _sb_prompt(q_sb, k_sb, v_sb, sb_bias[l])
    mem_k, mem_v = _proj(mem_prompt.reshape(bp * N_MEM, D_MODEL), w_mem_kv[l].astype(bf16),
                         (MEM_WIDTH, MEM_WIDTH), PROJ_TILE_M)
    mem_k = mem_k.reshape(bp, N_MEM, MEM_WIDTH)
    mem_v = mem_v.reshape(bp, N_MEM, MEM_WIDTH)
    h_p, shift_p, wkv_p = _branch_mix(
        x_prompt, [tok3(a) for a in rest], o_sb, mem_k, mem_v, jnp.zeros((bp, RW_SHIFT_WIDTH), f32),
        jnp.zeros((bp, RW_HEADS, RW_HEAD_DIM, RW_HEAD_DIM), f32), lw)

    xs = x_sample.reshape(bs * ss, D_MODEL)
    q_s, k_s, v_s, *rest_s = _proj(xs, w_in_b, IN_SPLITS, bs * ss)
    tok3s = lambda a: a.reshape(bs, ss, a.shape[-1])
    heads_s = lambda a: a.reshape(bs, ss, SB_HEADS, SB_HEAD_DIM)
    o_sb_s = _sb_sample(heads_s(q_s), heads_s(k_s), heads_s(v_s), sb_bias[l], cache_sb_k[l], cache_sb_v[l],
                        page_table)
    h_s, shift_s, wkv_s = _branch_mix(
        x_sample, [tok3s(a) for a in rest_s], o_sb_s, cache_mem_k[l].reshape(bs, N_MEM, MEM_WIDTH),
        cache_mem_v[l].reshape(bs, N_MEM, MEM_WIDTH), state_rw_shift[l], state_rw_wkv[l], lw)

    h_all = jnp.concatenate([h_p.reshape(bp * sp, D_MODEL), h_s.reshape(bs * ss, D_MODEL)], axis=0)
    picked, top_p = _moe(h_all, w_router[l], b_router[l], w_gate_up[l], b_gate_up[l], w_down[l], b_down[l])
    y_prompt = _combine(picked, top_p, h_all, ln2_g[l], ln2_b[l], 0, bp * sp).reshape(bp, sp, D_MODEL)
    y_sample = _combine(picked, top_p, h_all, ln2_g[l], ln2_b[l], bp * sp, bs * ss).reshape(bs, ss, D_MODEL)

    heads_m = lambda a: a.reshape(1, bp, N_MEM, MEM_HEADS, MEM_HEAD_DIM)
    return (y_prompt, y_sample,
            k_sb.reshape(1, bp, sp, SB_HEADS, SB_HEAD_DIM), v_sb.reshape(1, bp, sp, SB_HEADS, SB_HEAD_DIM),
            heads_m(mem_k), heads_m(mem_v), shift_p[None], wkv_p[None],
            k_s.reshape(1, bs, ss, SB_HEADS, SB_HEAD_DIM), v_s.reshape(1, bs, ss, SB_HEADS, SB_HEAD_DIM),
            shift_s[None], wkv_s[None])
```

```python
import functools

import jax
import jax.numpy as jnp
import numpy as np
from jax import lax
from jax.experimental import pallas as pl
from jax.experimental.pallas import tpu as pltpu

D_MODEL = 1024
PAGE_SIZE = 128
N_MEM = 256
SB_HEADS = 8
SB_HEAD_DIM = 64
SB_WIDTH = SB_HEADS * SB_HEAD_DIM
SB_SCALE = SB_HEAD_DIM ** -0.5
LOG2_E = 1.4426950408889634
MEM_HEADS = 4
MEM_HEAD_DIM = 128
MEM_WIDTH = MEM_HEADS * MEM_HEAD_DIM
MEM_SCALE = MEM_HEAD_DIM ** -0.5
RW_HEADS = 8
RW_HEAD_DIM = 64
RW_WIDTH = RW_HEADS * RW_HEAD_DIM
DECAY_LORA = 64
AAA_LORA = 64
GATE_LORA = 128
RW_SPLITS = (RW_WIDTH, RW_WIDTH, RW_WIDTH, DECAY_LORA, AAA_LORA, GATE_LORA)
RW_SHIFT_WIDTH = sum(RW_SPLITS)
RW_GN_EPS = 64e-5
N_EXPERTS = 32
TOP_K = 4
D_FF = D_MODEL
SWIGLU_ALPHA = 1.702
SWIGLU_LIMIT = 7.0
DEPTH = 1
DN_ALPHA = (2 * DEPTH) ** 0.25
LN_EPS = 1e-5
IN_SPLITS = (SB_WIDTH, SB_WIDTH, SB_WIDTH, MEM_WIDTH, RW_SHIFT_WIDTH, D_MODEL, D_MODEL, D_MODEL)
IN_WIDTH = sum(IN_SPLITS)

LANES = 128
MXU_DIM = 256
VMEM_LIMIT_BYTES = 48 * 1024 * 1024

SB_TILE = MXU_DIM
SB_STEP_WIDTH = 8 * SB_HEAD_DIM
PROJ_TILE_M = 256
MOE_TILE_M = 256

f32 = jnp.float32
bf16 = jnp.bfloat16


def _split(x, sizes):
    return jnp.split(x, np.cumsum(sizes)[:-1].tolist(), axis=-1)


def _proj_kernel(x_ref, w_ref, *o_refs, splits):
    xb = x_ref[...].astype(bf16)
    off = 0
    for o_ref, n in zip(o_refs, splits):
        o_ref[...] = jnp.dot(xb, w_ref[:, off:off + n], preferred_element_type=f32)
        off += n


def _proj(x, w_bf16, splits, tile_m):
    t, d = x.shape
    n = w_bf16.shape[1]
    assert t % tile_m == 0 and sum(splits) == n
    return pl.pallas_call(
        functools.partial(_proj_kernel, splits=splits),
        grid=(t // tile_m,),
        in_specs=[
            pl.BlockSpec((tile_m, d), lambda i: (i, 0)),
            pl.BlockSpec((d, n), lambda i: (0, 0), pipeline_mode=pl.Buffered(1)),
        ],
        out_specs=[pl.BlockSpec((tile_m, s), lambda i: (i, 0)) for s in splits],
        out_shape=[jax.ShapeDtypeStruct((t, s), f32) for s in splits],
        compiler_params=pltpu.CompilerParams(
            dimension_semantics=("arbitrary",), vmem_limit_bytes=VMEM_LIMIT_BYTES),
        name="proj_in",
    )(x, w_bf16)


def _sb_prompt_kernel(bias_ref, q_ref, k_ref, v_ref, o_ref, *, tile):
    hp = pl.program_id(1)
    qi = pl.program_id(2)
    heads = q_ref.shape[-1] // SB_HEAD_DIM
    row = lax.broadcasted_iota(jnp.int32, (tile, tile), 0)
    col = lax.broadcasted_iota(jnp.int32, (tile, tile), 1)
    later = (row > col).astype(bf16)
    causal = col < row
    q_all = q_ref[...] * (SB_SCALE * LOG2_E)
    qh = [q_all[:, hh * SB_HEAD_DIM:(hh + 1) * SB_HEAD_DIM].astype(bf16) for hh in range(heads)]
    bias = [bias_ref[hp * heads + hh] * LOG2_E for hh in range(heads)]

    def key_tile(j, state, masked):
        start = pl.multiple_of(j * tile, tile)
        hs = range(heads)
        lanes = [slice(h * SB_HEAD_DIM, (h + 1) * SB_HEAD_DIM) for h in hs]
        kh = [k_ref[pl.ds(start, tile), lanes[h]].astype(bf16) for h in hs]
        z2 = [lax.dot_general(qh[h], kh[h], (((1,), (1,)), ((), ())), preferred_element_type=f32) + bias[h]
              for h in hs]
        sp2 = [jnp.maximum(z2[h], 0.0) + jnp.log2(1.0 + jnp.exp2(-jnp.abs(z2[h]))) for h in hs]
        drop = [jnp.where(causal, sp2[h], 0.0) if masked else sp2[h] for h in hs]
        inner = [jnp.dot(drop[h].astype(bf16), later, preferred_element_type=f32) for h in hs]
        a = [jnp.exp2(z2[h] - (sp2[h] + state[h][0] + inner[h])) for h in hs]
        if masked:
            a = [jnp.where(causal, a[h], 0.0) for h in hs]
        vh = [v_ref[pl.ds(start, tile), lanes[h]].astype(bf16) for h in hs]
        acc = [state[h][1] + jnp.dot(a[h].astype(bf16), vh[h], preferred_element_type=f32) for h in hs]
        carry = [state[h][0] + jnp.sum(drop[h], axis=1, keepdims=True) for h in hs]
        return tuple((carry[h], acc[h]) for h in hs)

    state = tuple((jnp.zeros((tile, 1), f32), jnp.zeros((tile, SB_HEAD_DIM), f32)) for _ in range(heads))
    state = key_tile(qi, state, True)
    state = lax.fori_loop(0, qi, lambda s, st: key_tile(qi - 1 - s, st, False), state)
    o_ref[...] = jnp.concatenate([acc for _, acc in state], axis=1)


def _sb_prompt(q, k, v, bias, tile=SB_TILE):
    b, s, w = q.shape
    assert s % tile == 0 and w % SB_STEP_WIDTH == 0
    blk = lambda bb, hp, qi: (bb, qi, hp)
    full = lambda bb, hp, qi: (bb, 0, hp)
    return pl.pallas_call(
        functools.partial(_sb_prompt_kernel, tile=tile),
        grid=(b, w // SB_STEP_WIDTH, s // tile),
        in_specs=[
            pl.BlockSpec(memory_space=pltpu.SMEM),
            pl.BlockSpec((None, tile, SB_STEP_WIDTH), blk),
            pl.BlockSpec((None, s, SB_STEP_WIDTH), full),
            pl.BlockSpec((None, s, SB_STEP_WIDTH), full),
        ],
        out_specs=pl.BlockSpec((None, tile, SB_STEP_WIDTH), blk),
        out_shape=jax.ShapeDtypeStruct((b, s, w), f32),
        compiler_params=pltpu.CompilerParams(
            dimension_semantics=("arbitrary", "arbitrary", "arbitrary"),
            vmem_limit_bytes=VMEM_LIMIT_BYTES),
        name="sb_prompt",
    )(bias, q, k, v)


SBS_PAGES_PER_STEP = 16


def _sb_sample_kernel(pt_ref, bias_ref, q_ref, *refs, n_slots):
    del pt_ref
    k_refs = refs[:n_slots]
    v_refs = refs[n_slots:2 * n_slots]
    o_ref = refs[2 * n_slots]
    qb_ref, acc_ref, carry_ref, z_ref = refs[2 * n_slots + 1:]
    step = pl.program_id(1)

    @pl.when(step == 0)
    def _():
        q = q_ref[...] * SB_SCALE
        qb_ref[...] = jnp.broadcast_to(q, qb_ref.shape)
        acc_ref[...] = jnp.zeros_like(acc_ref)
        carry_ref[...] = jnp.zeros_like(carry_ref)

    row = lax.broadcasted_iota(jnp.int32, (PAGE_SIZE, PAGE_SIZE), 0)
    col = lax.broadcasted_iota(jnp.int32, (PAGE_SIZE, PAGE_SIZE), 1)
    later = (row > col).astype(bf16)
    later2 = jnp.concatenate([later, later], axis=0)

    for i, k_ref in enumerate(k_refs):
        for h in range(SB_HEADS):
            zh = jnp.sum(k_ref[h] * qb_ref[h], axis=0, keepdims=True) + bias_ref[h]
            z_ref[pl.ds(i * SB_HEADS + h, 1), :] = zh
    z = z_ref[...]
    softplus = jnp.maximum(z, 0.0) + jnp.log(1.0 + jnp.exp(-jnp.abs(z)))
    log_keep = -softplus
    hi = log_keep.astype(bf16)
    lo = (log_keep - hi.astype(f32)).astype(bf16)
    inner = jnp.dot(jnp.concatenate([hi, lo], axis=1), later2, preferred_element_type=f32)
    total = jnp.sum(log_keep, axis=1, keepdims=True)
    base = z - softplus + inner
    carry = carry_ref[...]
    for i in range(n_slots):
        sl = slice(i * SB_HEADS, (i + 1) * SB_HEADS)
        w = jnp.exp(base[sl] + carry)
        for h in range(SB_HEADS):
            acc_ref[h] += v_refs[i][h] * w[h:h + 1, :]
        carry = carry + total[sl]
    carry_ref[...] = carry

    @pl.when(step == pl.num_programs(1) - 1)
    def _():
        o_ref[...] = jnp.sum(acc_ref[...], axis=2, keepdims=True)


def _sb_sample_past(q, bias, k_pool, v_pool, page_table):
    b, n_pages = page_table.shape
    n_slots = SBS_PAGES_PER_STEP
    assert n_pages % n_slots == 0
    page_block = (None, SB_HEADS, SB_HEAD_DIM, PAGE_SIZE)

    def page_spec(slot):
        return pl.BlockSpec(
            page_block, lambda bb, s, pt, slot=slot: (pt[bb, n_pages - 1 - (s * n_slots + slot)], 0, 0, 0))

    grid_spec = pltpu.PrefetchScalarGridSpec(
        num_scalar_prefetch=1,
        grid=(b, n_pages // n_slots),
        in_specs=[
            pl.BlockSpec(memory_space=pltpu.SMEM),
            pl.BlockSpec((None, SB_HEADS, SB_HEAD_DIM, 1), lambda bb, s, pt: (bb, 0, 0, 0)),
        ] + [page_spec(i) for i in range(n_slots)] * 2,
        out_specs=pl.BlockSpec((None, SB_HEADS, SB_HEAD_DIM, 1), lambda bb, s, pt: (bb, 0, 0, 0)),
        scratch_shapes=[
            pltpu.VMEM((SB_HEADS, SB_HEAD_DIM, PAGE_SIZE), f32),
            pltpu.VMEM((SB_HEADS, SB_HEAD_DIM, PAGE_SIZE), f32),
            pltpu.VMEM((SB_HEADS, 1), f32),
            pltpu.VMEM((n_slots * SB_HEADS, PAGE_SIZE), f32),
        ],
    )
    o = pl.pallas_call(
        functools.partial(_sb_sample_kernel, n_slots=n_slots),
        grid_spec=grid_spec,
        out_shape=jax.ShapeDtypeStruct((b, SB_HEADS, SB_HEAD_DIM, 1), f32),
        compiler_params=pltpu.CompilerParams(
            dimension_semantics=("arbitrary", "arbitrary"), vmem_limit_bytes=VMEM_LIMIT_BYTES),
        name="sb_sample",
    )(page_table, bias, q[..., None], *([k_pool] * n_slots), *([v_pool] * n_slots))
    return o[..., 0]


def _token_tile(t):
    return max(m for m in range(8, 513, 8) if t % m == 0)


def _route_kernel(h_ref, whi_ref, wlo_ref, b_ref, e_ref, p_ref, cnt_ref, hb_ref):
    h = h_ref[...]
    h_hi = h.astype(bf16)
    hb_ref[...] = h_hi
    h_lo = (h - h_hi.astype(f32)).astype(bf16)
    dot = lambda a, w_ref: jnp.dot(a, w_ref[...], preferred_element_type=f32)
    work = dot(h_hi, whi_ref) + (dot(h_hi, wlo_ref) + dot(h_lo, whi_ref)) + b_ref[...]
    lane = lax.broadcasted_iota(jnp.int32, work.shape, 1)
    picked = jnp.zeros(work.shape, jnp.bool_)
    top_e, top_l = [], []
    for _ in range(TOP_K):
        m = jnp.max(work, axis=1, keepdims=True)
        idx = jnp.min(jnp.where(work == m, lane, N_EXPERTS), axis=1, keepdims=True)
        hit = lane == idx
        picked = picked | hit
        work = jnp.where(hit, -jnp.inf, work)
        top_e.append(idx)
        top_l.append(m)
    ex = [jnp.exp(l - top_l[0]) for l in top_l]
    den = ex[0] + ex[1] + ex[2] + ex[3]
    e_ref[...] = jnp.concatenate(top_e, axis=1)
    p_ref[...] = jnp.concatenate([x / den for x in ex], axis=1)
    cnt_ref[...] = jnp.sum(picked.astype(f32), axis=0, keepdims=True).astype(jnp.int32)


def _route(h, w_router, b_router):
    t, d = h.shape
    tile = _token_tile(t)
    w_hi = w_router.astype(bf16)
    w_lo = (w_router - w_hi.astype(f32)).astype(bf16)
    const = lambda a: pl.BlockSpec(a.shape, lambda i: (0,) * a.ndim)
    b2 = b_router[None, :]
    return pl.pallas_call(
        _route_kernel,
        grid=(t // tile,),
        in_specs=[pl.BlockSpec((tile, d), lambda i: (i, 0)), const(w_hi), const(w_lo), const(b2)],
        out_specs=[pl.BlockSpec((tile, TOP_K), lambda i: (i, 0)), pl.BlockSpec((tile, TOP_K), lambda i: (i, 0)),
                   pl.BlockSpec((None, 1, N_EXPERTS), lambda i: (i, 0, 0)), pl.BlockSpec((tile, d), lambda i: (i, 0))],
        out_shape=[jax.ShapeDtypeStruct((t, TOP_K), jnp.int32), jax.ShapeDtypeStruct((t, TOP_K), f32),
                   jax.ShapeDtypeStruct((t // tile, 1, N_EXPERTS), jnp.int32), jax.ShapeDtypeStruct((t, d), bf16)],
        compiler_params=pltpu.CompilerParams(
            dimension_semantics=("arbitrary",), vmem_limit_bytes=VMEM_LIMIT_BYTES),
        name="moe_route",
    )(h, w_hi, w_lo, b2)


def _pair_pos_kernel(e_ref, base_ref, pos_ref):
    tile = e_ref.shape[0]
    lane = lax.broadcasted_iota(jnp.int32, (tile, N_EXPERTS), 1)
    hits = [lane == e_ref[:, j:j + 1] for j in range(TOP_K)]
    any_hit = hits[0] | hits[1] | hits[2] | hits[3]
    row = lax.broadcasted_iota(jnp.int32, (tile, tile), 0)
    col = lax.broadcasted_iota(jnp.int32, (tile, tile), 1)
    earlier = (col < row).astype(bf16)
    before = jnp.dot(earlier, any_hit.astype(bf16), preferred_element_type=f32) + base_ref[...].astype(f32)
    pos = [jnp.sum(jnp.where(hit, before, 0.0), axis=1, keepdims=True) for hit in hits]
    pos_ref[...] = jnp.concatenate(pos, axis=1).astype(jnp.int32)


def _pair_pos(top_e, tile_base):
    t = top_e.shape[0]
    n_tiles = tile_base.shape[0]
    tile = t // n_tiles
    return pl.pallas_call(
        _pair_pos_kernel,
        grid=(n_tiles,),
        in_specs=[pl.BlockSpec((tile, TOP_K), lambda i: (i, 0)),
                  pl.BlockSpec((None, 1, N_EXPERTS), lambda i: (i, 0, 0))],
        out_specs=pl.BlockSpec((tile, TOP_K), lambda i: (i, 0)),
        out_shape=jax.ShapeDtypeStruct((t, TOP_K), jnp.int32),
        compiler_params=pltpu.CompilerParams(
            dimension_semantics=("arbitrary",), vmem_limit_bytes=VMEM_LIMIT_BYTES),
        name="moe_pair_pos",
    )(top_e, tile_base)


def _moe_kernel(tile_ref, exp_ref, lo_ref, hi_ref, first_ref, x_ref, wgu_ref, bg_ref, bl_ref, wd_ref, bd_ref,
                o_ref, wg_s, wl_s, wd_s):
    del tile_ref
    i = pl.program_id(0)
    ff = wd_ref.shape[0]
    half = MXU_DIM // 2

    src = lax.broadcasted_iota(jnp.int32, (MXU_DIM, MXU_DIM), 0)
    dst = lax.broadcasted_iota(jnp.int32, (MXU_DIM, MXU_DIM), 1)
    unzip = (src == jnp.where(dst < half, 2 * dst, 2 * (dst - half) + 1)).astype(bf16)

    @pl.when((i == 0) | (exp_ref[i] != exp_ref[jnp.maximum(i - 1, 0)]))
    def _():
        for c in range(2 * ff // MXU_DIM):
            cols = slice(c * MXU_DIM, (c + 1) * MXU_DIM)
            out = slice(c * half, (c + 1) * half)
            w = jnp.dot(wgu_ref[:, cols].astype(bf16), unzip, preferred_element_type=f32).astype(bf16)
            wg_s[:, out] = w[:, :half]
            wl_s[:, out] = w[:, half:]
        wd_s[...] = wd_ref[...].astype(bf16)

    @pl.when(first_ref[i] == 1)
    def _():
        o_ref[...] = jnp.zeros_like(o_ref)

    @pl.when(hi_ref[i] > lo_ref[i])
    def _():
        x = x_ref[...]
        g = jnp.dot(x, wg_s[...], preferred_element_type=f32) + bg_ref[...]
        l = jnp.dot(x, wl_s[...], preferred_element_type=f32) + bl_ref[...]
        glu = jnp.minimum(g, SWIGLU_LIMIT)
        lin = jnp.clip(l, -SWIGLU_LIMIT, SWIGLU_LIMIT)
        h = glu * (1.0 / (1.0 + jnp.exp(-SWIGLU_ALPHA * glu))) * (lin + 1.0)
        y = jnp.dot(h.astype(bf16), wd_s[...], preferred_element_type=f32) + bd_ref[...]
        row = lax.broadcasted_iota(jnp.int32, (x.shape[0], 1), 0)
        mine = (row >= lo_ref[i]) & (row < hi_ref[i])
        o_ref[...] = jnp.where(mine, y.astype(o_ref.dtype), o_ref[...])


def _moe_experts(xs, items, w_gate_up, b_gate_up, w_down, b_down, tile_m):
    n_rows, d = xs.shape
    ff = w_down.shape[1]
    n_items = items[0].shape[0]
    row = lambda i, tl, ex, lo, hi, fi: (tl[i], 0)
    wsel = lambda i, tl, ex, lo, hi, fi: (ex[i], 0, 0)
    grid_spec = pltpu.PrefetchScalarGridSpec(
        num_scalar_prefetch=5,
        grid=(n_items,),
        in_specs=[
            pl.BlockSpec((tile_m, d), row),
            pl.BlockSpec((None, d, 2 * ff), wsel),
            pl.BlockSpec((None, 1, ff), wsel),
            pl.BlockSpec((None, 1, ff), wsel),
            pl.BlockSpec((None, ff, d), wsel),
            pl.BlockSpec((None, 1, d), wsel),
        ],
        out_specs=pl.BlockSpec((tile_m, d), row),
        scratch_shapes=[
            pltpu.VMEM((d, ff), bf16), pltpu.VMEM((d, ff), bf16), pltpu.VMEM((ff, d), bf16),
        ],
    )
    return pl.pallas_call(
        _moe_kernel,
        grid_spec=grid_spec,
        out_shape=jax.ShapeDtypeStruct((n_rows, d), bf16),
        compiler_params=pltpu.CompilerParams(
            dimension_semantics=("arbitrary",), vmem_limit_bytes=VMEM_LIMIT_BYTES),
        name="moe_experts",
    )(*items, xs, w_gate_up, b_gate_up[:, None, 0::2], b_gate_up[:, None, 1::2], w_down, b_down[:, None, :])


def _moe_work_items(starts, n_tiles, tile_m):
    counts = starts[1:] - starts[:-1]
    first_tile = starts[:-1] // tile_m
    n_t = jnp.where(counts > 0, (starts[1:] - 1) // tile_m - first_tile + 1, 0)
    item_end = jnp.cumsum(n_t)
    item_start = item_end - n_t
    n_items = n_tiles + N_EXPERTS - 1
    i = jnp.arange(n_items, dtype=jnp.int32)
    valid = i < item_end[-1]
    ii = jnp.minimum(i, item_end[-1] - 1)
    exp = jnp.sum((ii[:, None] >= item_end[None, :]).astype(jnp.int32), axis=1)
    onehot = (exp[:, None] == jnp.arange(N_EXPERTS, dtype=jnp.int32)[None, :]).astype(jnp.int32)
    pick = lambda table: jnp.sum(onehot * table[None, :], axis=1)
    tile = pick(first_tile) + ii - pick(item_start)
    lo = jnp.clip(pick(starts[:-1]) - tile * tile_m, 0, tile_m)
    hi = jnp.clip(pick(starts[1:]) - tile * tile_m, 0, tile_m)
    hi = jnp.where(valid, hi, lo)
    first = jnp.concatenate([jnp.ones((1,), jnp.int32), (tile[1:] != tile[:-1]).astype(jnp.int32)])
    return tuple(a.astype(jnp.int32) for a in (tile, exp, lo, hi, first))


def _moe(h, w_router, b_router, w_gate_up, b_gate_up, w_down, b_down, tile_m=MOE_TILE_M):
    t = h.shape[0]
    n_pairs = t * TOP_K
    n_tiles = -(-n_pairs // tile_m)
    top_e, top_p, tile_counts, h_bf16 = _route(h, w_router, b_router)
    counts = jnp.sum(tile_counts[:, 0, :], axis=0)
    starts = jnp.concatenate([jnp.zeros((1,), jnp.int32), jnp.cumsum(counts)]).astype(jnp.int32)
    tile_base = (jnp.cumsum(tile_counts, axis=0) - tile_counts + starts[None, None, :-1]).astype(jnp.int32)
    pos = _pair_pos(top_e, tile_base)
    bits = max(1, (n_pairs - 1).bit_length())
    assert N_EXPERTS << bits < 2 ** 31
    keys = (top_e.reshape(-1) << bits) | jnp.arange(n_pairs, dtype=jnp.int32)
    order = jnp.sort(keys) & ((1 << bits) - 1)
    row_tok = jnp.concatenate([order // TOP_K, jnp.zeros((n_tiles * tile_m - n_pairs,), jnp.int32)])
    tall = jnp.concatenate([h_bf16, jnp.zeros((n_tiles * tile_m - t + 8, h.shape[1]), bf16)], axis=0)
    tall, idx = lax.optimization_barrier((tall, row_tok))
    xs = lax.optimization_barrier(tall[idx])
    ys = _moe_experts(xs, _moe_work_items(starts, n_tiles, tile_m), w_gate_up, b_gate_up, w_down, b_down, tile_m)
    picked = [lax.optimization_barrier(ys[pos[:, j]]) for j in range(TOP_K)]
    return picked, top_p


RW_LANE_HALF = LANES // 2
RW_KJ = RW_HEAD_DIM // 2
RW_VG = RW_HEAD_DIM // 8
RW_STEPS_PER_BLOCK = 32


def _rwkv_scan_kernel(kk_ref, d_ref, kka_ref, km_ref, r_ref, v_ref, s0_ref, y_ref, s_ref, *, steps):
    tb = pl.program_id(1)

    @pl.when(tb == 0)
    def _():
        s_ref[...] = s0_ref[...]

    def fold(x):
        return x + pltpu.roll(x, RW_LANE_HALF, axis=1)

    def project(kk_rows):
        acc = [None] * RW_VG
        for j in range(RW_KJ):
            kkj = kk_rows(j)
            for g in range(RW_VG):
                p = s_ref[g, j] * kkj
                acc[g] = p if acc[g] is None else acc[g] + p
        return tuple(acc)

    def step(t, acc, t_next):
        sa = [-fold(a) for a in acc]
        vv = [v_ref[t, g] for g in range(RW_VG)]
        yacc = [None] * RW_VG
        nxt = [None] * RW_VG
        for j in range(RW_KJ):
            dj = d_ref[t, pl.ds(j, 1), :]
            kkaj = kka_ref[t, pl.ds(j, 1), :]
            kmj = km_ref[t, pl.ds(j, 1), :]
            rj = r_ref[t, pl.ds(j, 1), :]
            kkn = None if t_next is None else kk_ref[t_next, pl.ds(j, 1), :]
            for g in range(RW_VG):
                s = s_ref[g, j] * dj + sa[g] * kkaj + vv[g] * kmj
                s_ref[g, j] = s
                p = s * rj
                yacc[g] = p if yacc[g] is None else yacc[g] + p
                if kkn is not None:
                    q = s * kkn
                    nxt[g] = q if nxt[g] is None else nxt[g] + q
        for g in range(RW_VG):
            y_ref[t, g] = fold(yacc[g])
        return tuple(nxt)

    acc = project(lambda j: kk_ref[0, pl.ds(j, 1), :])
    acc = lax.fori_loop(0, steps - 1, lambda t, a: step(t, a, t + 1), acc)
    step(steps - 1, acc, None)


def _rwkv_scan(kk, d, kka, km, r, v, s0):
    n_g, s_len = kk.shape[:2]
    steps = min(RW_STEPS_PER_BLOCK, s_len)
    assert s_len % steps == 0
    krow = pl.BlockSpec((None, steps, RW_KJ, LANES), lambda g, t: (g, t, 0, 0))
    vrow = pl.BlockSpec((None, steps, RW_VG, 8, LANES), lambda g, t: (g, t, 0, 0, 0))
    st = pl.BlockSpec((None, RW_VG, RW_KJ, 8, LANES), lambda g, t: (g, 0, 0, 0, 0))
    return pl.pallas_call(
        functools.partial(_rwkv_scan_kernel, steps=steps),
        grid=(n_g, s_len // steps),
        in_specs=[krow, krow, krow, krow, krow, vrow, st],
        out_specs=[vrow, st],
        out_shape=[jax.ShapeDtypeStruct(v.shape, f32), jax.ShapeDtypeStruct(s0.shape, f32)],
        compiler_params=pltpu.CompilerParams(
            dimension_semantics=("arbitrary", "arbitrary"), vmem_limit_bytes=VMEM_LIMIT_BYTES),
        name="rwkv_scan",
    )(kk, d, kka, km, r, v, s0)


def _rw_groups(n_bh):
    assert n_bh % RW_LANE_HALF == 0
    return n_bh // RW_LANE_HALF


def _rw_k_layout(x):
    b, s, h, _ = x.shape
    g = _rw_groups(b * h)
    x = x.reshape(b, s, h, 2, RW_KJ)
    x = jnp.transpose(x, (1, 4, 3, 0, 2)).reshape(s, RW_KJ, 2, g, RW_LANE_HALF)
    return jnp.transpose(x, (3, 0, 1, 2, 4)).reshape(g, s, RW_KJ, LANES)


def _rw_v_layout(x):
    b, s, h, _ = x.shape
    g = _rw_groups(b * h)
    x = jnp.transpose(x, (1, 3, 0, 2)).reshape(s, RW_HEAD_DIM, g, RW_LANE_HALF)
    x = jnp.transpose(x, (2, 0, 1, 3))
    x = jnp.concatenate([x, x], axis=-1)
    return x.reshape(g, s, RW_VG, 8, LANES)


def _rw_v_unlayout(y, b, h):
    g, s = y.shape[:2]
    y = y.reshape(g, s, RW_HEAD_DIM, LANES)[..., :RW_LANE_HALF]
    y = jnp.transpose(y, (1, 2, 0, 3)).reshape(s, RW_HEAD_DIM, b, h)
    return jnp.transpose(y, (2, 0, 3, 1))


def _rw_state_layout(wkv):
    b, h = wkv.shape[:2]
    g = _rw_groups(b * h)
    x = wkv.reshape(g, RW_LANE_HALF, RW_VG, 8, 2, RW_KJ)
    return jnp.transpose(x, (0, 2, 5, 3, 4, 1)).reshape(g, RW_VG, RW_KJ, 8, LANES)


def _rw_state_unlayout(st, b, h):
    g = st.shape[0]
    x = st.reshape(g, RW_VG, RW_KJ, 8, 2, RW_LANE_HALF)
    x = jnp.transpose(x, (0, 5, 1, 3, 4, 2))
    return x.reshape(b, h, RW_HEAD_DIM, RW_HEAD_DIM)


MIX_TILE_M = 256


def _layer_norm_rows(pre, g, b):
    mu = jnp.mean(pre, axis=1, keepdims=True)
    cen = pre - mu
    var = jnp.mean(cen * cen, axis=1, keepdims=True)
    return cen * lax.rsqrt(var + LN_EPS) * g + b


def _sigmoid(x):
    return 1.0 / (1.0 + jnp.exp(-x))


def _mem_attend(q, mk_ref, mv_ref):
    heads = []
    for hd in range(MEM_HEADS):
        lanes = slice(hd * MEM_HEAD_DIM, (hd + 1) * MEM_HEAD_DIM)
        s = lax.dot_general(q[:, lanes], mk_ref[:, lanes].astype(bf16), (((1,), (1,)), ((), ())),
                            preferred_element_type=f32) * MEM_SCALE
        p = jnp.exp(s - jnp.max(s, axis=1, keepdims=True))
        p = p / jnp.sum(p, axis=1, keepdims=True)
        heads.append(jnp.dot(p.astype(bf16), mv_ref[:, lanes].astype(bf16), preferred_element_type=f32))
    return jnp.concatenate(heads, axis=1)


def _mem_attend_kernel(q_ref, mk_ref, mv_ref, o_ref):
    o_ref[...] = _mem_attend(q_ref[...].astype(bf16), mk_ref, mv_ref)


def _mem_attend_tokens(q_mem, mem_k, mem_v):
    b, s, w = q_mem.shape
    tok = pl.BlockSpec((None, s, w), lambda bb: (bb, 0, 0))
    mem = pl.BlockSpec((None, N_MEM, MEM_WIDTH), lambda bb: (bb, 0, 0))
    return pl.pallas_call(
        _mem_attend_kernel,
        grid=(b,),
        in_specs=[tok, mem, mem],
        out_specs=tok,
        out_shape=jax.ShapeDtypeStruct((b, s, w), f32),
        compiler_params=pltpu.CompilerParams(
            dimension_semantics=("arbitrary",), vmem_limit_bytes=VMEM_LIMIT_BYTES),
        name="mem_attend",
    )(q_mem, mem_k, mem_v)


def _mix_kernel(*refs, attend):
    if attend:
        x_ref, osb_ref, third_ref, orw_ref, gsb_ref, gmem_ref, grw_ref, mk_ref, mv_ref = refs[:9]
        o_mem = _mem_attend(third_ref[...].astype(bf16), mk_ref, mv_ref)
    else:
        x_ref, osb_ref, third_ref, orw_ref, gsb_ref, gmem_ref, grw_ref = refs[:7]
        o_mem = third_ref[...]
    wsb_ref, wmem_ref, wrw_ref, wout_ref, g1_ref, b1_ref, h_ref = refs[-7:]
    branch = lambda o, w_ref: jnp.dot(o.astype(bf16), w_ref[...], preferred_element_type=f32)
    merged = (_sigmoid(gsb_ref[...]) * branch(osb_ref[...], wsb_ref)
              + _sigmoid(gmem_ref[...]) * branch(o_mem, wmem_ref)
              + _sigmoid(grw_ref[...]) * branch(orw_ref[...], wrw_ref))
    pre = DN_ALPHA * x_ref[...] + branch(merged, wout_ref)
    h_ref[...] = _layer_norm_rows(pre, g1_ref[...], b1_ref[...])


def _mix(x, o_sb, q_or_o_mem, o_rw, g_sb, g_mem, g_rw, mem_kv, w_sb_o, w_mem_o, w_rw_o, w_out, ln_g, ln_b):
    b, s, d = x.shape
    tile = min(MIX_TILE_M, s)
    assert s % tile == 0
    tok = lambda w: pl.BlockSpec((None, tile, w), lambda bb, i: (bb, i, 0))
    mem = pl.BlockSpec((None, N_MEM, MEM_WIDTH), lambda bb, i: (bb, 0, 0))
    const = lambda a: pl.BlockSpec(a.shape, lambda bb, i: (0,) * a.ndim)
    weights = [w.astype(bf16) for w in (w_sb_o, w_mem_o, w_rw_o, w_out)] + [ln_g[None, :], ln_b[None, :]]
    mems = () if mem_kv is None else tuple(mem_kv)
    return pl.pallas_call(
        functools.partial(_mix_kernel, attend=mem_kv is not None),
        grid=(b, s // tile),
        in_specs=[tok(d), tok(SB_WIDTH), tok(MEM_WIDTH), tok(RW_WIDTH), tok(d), tok(d), tok(d)]
        + [mem] * len(mems) + [const(w) for w in weights],
        out_specs=tok(d),
        out_shape=jax.ShapeDtypeStruct((b, s, d), f32),
        compiler_params=pltpu.CompilerParams(
            dimension_semantics=("arbitrary", "arbitrary"), vmem_limit_bytes=VMEM_LIMIT_BYTES),
        name="mix_ln1",
    )(x, o_sb, q_or_o_mem, o_rw, g_sb, g_mem, g_rw, *mems, *weights)


def _combine_kernel(*refs):
    pick_refs = refs[:TOP_K]
    p_ref, h_ref, g_ref, b_ref, y_ref = refs[TOP_K:]
    p = p_ref[...]
    moe = pick_refs[0][...].astype(f32) * p[:, 0:1]
    for j in range(1, TOP_K):
        moe = moe + pick_refs[j][...].astype(f32) * p[:, j:j + 1]
    y_ref[...] = _layer_norm_rows(DN_ALPHA * h_ref[...] + moe, g_ref[...], b_ref[...])


def _combine(picked, top_p, h, ln_g, ln_b, first_row, n_rows):
    d = h.shape[1]
    tile = min(MIX_TILE_M, n_rows)
    assert n_rows % tile == 0 and first_row % tile == 0
    first = first_row // tile
    row = lambda w: pl.BlockSpec((tile, w), lambda i: (first + i, 0))
    vec = pl.BlockSpec((1, d), lambda i: (0, 0))
    return pl.pallas_call(
        _combine_kernel,
        grid=(n_rows // tile,),
        in_specs=[row(d)] * TOP_K + [row(TOP_K), row(d), vec, vec],
        out_specs=pl.BlockSpec((tile, d), lambda i: (i, 0)),
        out_shape=jax.ShapeDtypeStruct((n_rows, d), f32),
        compiler_params=pltpu.CompilerParams(
            dimension_semantics=("arbitrary",), vmem_limit_bytes=VMEM_LIMIT_BYTES),
        name="combine_ln2",
    )(*picked, top_p, h, ln_g[None, :], ln_b[None, :])


def _sb_logits(q, k, bias):
    z = jnp.einsum('bqhd,bkhd->bhqk', q, k, preferred_element_type=f32) * SB_SCALE
    return z + bias.astype(f32)[None, :, None, None]


def _sb_weights(z, causal):
    log_keep = jnp.where(causal, jax.nn.log_sigmoid(-z), 0.0)
    log_survive = lax.cumsum(log_keep, axis=z.ndim - 1, reverse=True) - log_keep
    return jnp.where(causal, jnp.exp(jax.nn.log_sigmoid(z) + log_survive), 0.0)


def _sb_sample(q, k_new, v_new, bias, k_pool, v_pool, page_table):
    b, s = q.shape[:2]
    assert s == 1
    to_token_minor = lambda pool: jnp.transpose(pool, (0, 2, 3, 1))
    o_past = _sb_sample_past(q[:, 0], bias, to_token_minor(k_pool), to_token_minor(v_pool), page_table)
    new_pos = jnp.arange(s)
    w_new = _sb_weights(_sb_logits(q, k_new, bias), new_pos[None, :] < new_pos[:, None])
    o_new = jnp.einsum('bhqk,bkhd->bqhd', w_new, v_new)
    return (o_past[:, None] + o_new).reshape(b, s, SB_WIDTH)


def _rwkv7(p_rw, shift0, wkv0, mu_rw, w_decay0, w_decay2, w_aaa0, w_aaa2, w_gate2,
           rw_k_k, rw_k_a, rw_r_k, rw_gn_g, rw_gn_b):
    b, s, _ = p_rw.shape
    prev = jnp.concatenate([shift0[:, None, :], p_rw[:, :-1]], axis=1)
    xs = p_rw + mu_rw * (prev - p_rw)
    r, k, v, w_lo, a_lo, g_lo = _split(xs, RW_SPLITS)
    w_raw = w_decay0 + jnp.tanh(w_lo) @ w_decay2
    decay = jnp.exp(-jnp.exp(-jax.nn.softplus(-w_raw) - 0.5))
    a = jax.nn.sigmoid(w_aaa0 + a_lo @ w_aaa2)
    g = jax.nn.sigmoid(g_lo) @ w_gate2
    heads = lambda t: t.reshape(b, s, RW_HEADS, RW_HEAD_DIM)
    kk = heads(k * rw_k_k)
    kk = kk / jnp.maximum(jnp.sqrt(jnp.sum(kk * kk, axis=-1, keepdims=True)), 1e-12)
    k_mod = k * (1.0 + (a - 1.0) * rw_k_a)
    rh, kh, vh, ah, dh = heads(r), heads(k_mod), heads(v), heads(a), heads(decay)

    y, wkv = _rwkv_scan(_rw_k_layout(kk), _rw_k_layout(dh), _rw_k_layout(kk * ah), _rw_k_layout(kh),
                        _rw_k_layout(rh), _rw_v_layout(vh), _rw_state_layout(wkv0))
    y = _rw_v_unlayout(y, b, RW_HEADS)
    wkv = _rw_state_unlayout(wkv, b, RW_HEADS)
    mu = y.mean(-1, keepdims=True)
    var = jnp.square(y - mu).mean(-1, keepdims=True)
    y = ((y - mu) * lax.rsqrt(var + RW_GN_EPS)).reshape(b, s, RW_WIDTH) * rw_gn_g + rw_gn_b
    bonus = jnp.sum(rh * kh * rw_r_k, axis=-1, keepdims=True) * vh
    y = (y + bonus.reshape(b, s, RW_WIDTH)) * g
    return y, p_rw[:, -1], wkv


def _branch_mix(x, parts, o_sb, mem_k, mem_v, shift0, wkv0, lw):
    q_mem, p_rw, g_sb, g_mem, g_rw = parts
    o_rw, shift, wkv = _rwkv7(p_rw, shift0, wkv0, lw['mu_rw'], lw['w_decay0'], lw['w_decay2'],
                              lw['w_aaa0'], lw['w_aaa2'], lw['w_gate2'], lw['rw_k_k'], lw['rw_k_a'],
                              lw['rw_r_k'], lw['rw_gn_g'], lw['rw_gn_b'])
    tail = (lw['w_sb_o'], lw['w_mem_o'], lw['w_rw_o'], lw['w_out'], lw['ln1_g'], lw['ln1_b'])
    if x.shape[1] == 1:
        rows = lambda a: a.reshape(1, a.shape[0], a.shape[2])
        o_mem = _mem_attend_tokens(q_mem, mem_k, mem_v)
        h = _mix(rows(x), rows(o_sb), rows(o_mem), rows(o_rw), rows(g_sb), rows(g_mem), rows(g_rw), None, *tail)
        h = h.reshape(x.shape)
    else:
        h = _mix(x, o_sb, q_mem, o_rw, g_sb, g_mem, g_rw, (mem_k, mem_v), *tail)
    return h, shift, wkv


def kernel(x_prompt, x_sample, cache_sb_k, cache_sb_v, cache_mem_k, cache_mem_v, state_rw_shift,
           state_rw_wkv, page_table, mem_prompt, w_in, sb_bias, mu_rw, w_decay0, w_decay2, w_aaa0,
           w_aaa2, w_gate2, rw_k_k, rw_k_a, rw_r_k, rw_gn_g, rw_gn_b, w_mem_kv, w_sb_o, w_mem_o,
           w_rw_o, w_out, ln1_g, ln1_b, w_router, b_router, w_gate_up, b_gate_up, w_down, b_down,
           ln2_g, ln2_b):
    assert w_in.shape[0] == DEPTH == 1
    l = 0
    lw = dict(mu_rw=mu_rw[l], w_decay0=w_decay0[l], w_decay2=w_decay2[l], w_aaa0=w_aaa0[l],
              w_aaa2=w_aaa2[l], w_gate2=w_gate2[l], rw_k_k=rw_k_k[l], rw_k_a=rw_k_a[l],
              rw_r_k=rw_r_k[l], rw_gn_g=rw_gn_g[l], rw_gn_b=rw_gn_b[l], w_sb_o=w_sb_o[l],
              w_mem_o=w_mem_o[l], w_rw_o=w_rw_o[l], w_out=w_out[l], ln1_g=ln1_g[l], ln1_b=ln1_b[l])
    bp, sp, _ = x_prompt.shape
    bs, ss, _ = x_sample.shape
    w_in_b = w_in[l].astype(bf16)

    xp = x_prompt.reshape(bp * sp, D_MODEL)
    q_sb, k_sb, v_sb, *rest = _proj(xp, w_in_b, IN_SPLITS, PROJ_TILE_M)
    tok3 = lambda a: a.reshape(bp, sp, a.shape[-1])
    q_sb, k_sb, v_sb = tok3(q_sb), tok3(k_sb), tok3(v_sb)
    o_sb = _sb_prompt(q_sb, k_sb, v_sb, sb_bias[l])
    mem_k, mem_v = _proj(mem_prompt.reshape(bp * N_MEM, D_MODEL), w_mem_kv[l].astype(bf16),
                         (MEM_WIDTH, MEM_WIDTH), PROJ_TILE_M)
    mem_k = mem_k.reshape(bp, N_MEM, MEM_WIDTH)
    mem_v = mem_v.reshape(bp, N_MEM, MEM_WIDTH)
    h_p, shift_p, wkv_p = _branch_mix(
        x_prompt, [tok3(a) for a in rest], o_sb, mem_k, mem_v, jnp.zeros((bp, RW_SHIFT_WIDTH), f32),
        jnp.zeros((bp, RW_HEADS, RW_HEAD_DIM, RW_HEAD_DIM), f32), lw)

    xs = x_sample.reshape(bs * ss, D_MODEL)
    q_s, k_s, v_s, *rest_s = _proj(xs, w_in_b, IN_SPLITS, bs * ss)
    tok3s = lambda a: a.reshape(bs, ss, a.shape[-1])
    heads_s = lambda a: a.reshape(bs, ss, SB_HEADS, SB_HEAD_DIM)
    o_sb_s = _sb_sample(heads_s(q_s), heads_s(k_s), heads_s(v_s), sb_bias[l], cache_sb_k[l], cache_sb_v[l],
                        page_table)
    h_s, shift_s, wkv_s = _branch_mix(
        x_sample, [tok3s(a) for a in rest_s], o_sb_s, cache_mem_k[l].reshape(bs, N_MEM, MEM_WIDTH),
        cache_mem_v[l].reshape(bs, N_MEM, MEM_WIDTH), state_rw_shift[l], state_rw_wkv[l], lw)

    h_all = jnp.concatenate([h_p.reshape(bp * sp, D_MODEL), h_s.reshape(bs * ss, D_MODEL)], axis=0)
    picked, top_p = _moe(h_all, w_router[l], b_router[l], w_gate_up[l], b_gate_up[l], w_down[l], b_down[l])
    y_prompt = _combine(picked, top_p, h_all, ln2_g[l], ln2_b[l], 0, bp * sp).reshape(bp, sp, D_MODEL)
    y_sample = _combine(picked, top_p, h_all, ln2_g[l], ln2_b[l], bp * sp, bs * ss).reshape(bs, ss, D_MODEL)

    heads_m = lambda a: a.reshape(1, bp, N_MEM, MEM_HEADS, MEM_HEAD_DIM)
    return (y_prompt, y_sample,
            k_sb.reshape(1, bp, sp, SB_HEADS, SB_HEAD_DIM), v_sb.reshape(1, bp, sp, SB_HEADS, SB_HEAD_DIM),
            heads_m(mem_k), heads_m(mem_v), shift_p[None], wkv_p[None],
            k_s.reshape(1, bs, ss, SB_HEADS, SB_HEAD_DIM), v_s.reshape(1, bs, ss, SB_HEADS, SB_HEAD_DIM),
            shift_s[None], wkv_s[None])
```

```python
import functools

import jax
import jax.numpy as jnp
import numpy as np
from jax import lax
from jax.experimental import pallas as pl
from jax.experimental.pallas import tpu as pltpu

D_MODEL = 1024
PAGE_SIZE = 128
N_MEM = 256
SB_HEADS = 8
SB_HEAD_DIM = 64
SB_WIDTH = SB_HEADS * SB_HEAD_DIM
SB_SCALE = SB_HEAD_DIM ** -0.5
LOG2_E = 1.4426950408889634
MEM_HEADS = 4
MEM_HEAD_DIM = 128
MEM_WIDTH = MEM_HEADS * MEM_HEAD_DIM
MEM_SCALE = MEM_HEAD_DIM ** -0.5
RW_HEADS = 8
RW_HEAD_DIM = 64
RW_WIDTH = RW_HEADS * RW_HEAD_DIM
DECAY_LORA = 64
AAA_LORA = 64
GATE_LORA = 128
RW_SPLITS = (RW_WIDTH, RW_WIDTH, RW_WIDTH, DECAY_LORA, AAA_LORA, GATE_LORA)
RW_SHIFT_WIDTH = sum(RW_SPLITS)
RW_GN_EPS = 64e-5
N_EXPERTS = 32
TOP_K = 4
D_FF = D_MODEL
SWIGLU_ALPHA = 1.702
SWIGLU_LIMIT = 7.0
DEPTH = 1
DN_ALPHA = (2 * DEPTH) ** 0.25
LN_EPS = 1e-5
IN_SPLITS = (SB_WIDTH, SB_WIDTH, SB_WIDTH, MEM_WIDTH, RW_SHIFT_WIDTH, D_MODEL, D_MODEL, D_MODEL)
IN_WIDTH = sum(IN_SPLITS)

LANES = 128
SUBLANES = 8
MXU_DIM = 256
VMEM_LIMIT_BYTES = 48 * 1024 * 1024

SB_TILE = MXU_DIM
SB_STEP_WIDTH = 8 * SB_HEAD_DIM
PROJ_TILE_M = 256
MOE_TILE_M = 256

f32 = jnp.float32
bf16 = jnp.bfloat16


def _split(x, sizes):
    return jnp.split(x, np.cumsum(sizes)[:-1].tolist(), axis=-1)


def _proj_kernel(x_ref, w_ref, *o_refs, splits):
    xb = x_ref[...].astype(bf16)
    off = 0
    for o_ref, n in zip(o_refs, splits):
        o_ref[...] = jnp.dot(xb, w_ref[:, off:off + n], preferred_element_type=f32)
        off += n


def _proj(x, w_bf16, splits, tile_m):
    t, d = x.shape
    n = w_bf16.shape[1]
    assert t % tile_m == 0 and sum(splits) == n
    return pl.pallas_call(
        functools.partial(_proj_kernel, splits=splits),
        grid=(t // tile_m,),
        in_specs=[
            pl.BlockSpec((tile_m, d), lambda i: (i, 0)),
            pl.BlockSpec((d, n), lambda i: (0, 0), pipeline_mode=pl.Buffered(1)),
        ],
        out_specs=[pl.BlockSpec((tile_m, s), lambda i: (i, 0)) for s in splits],
        out_shape=[jax.ShapeDtypeStruct((t, s), f32) for s in splits],
        compiler_params=pltpu.CompilerParams(
            dimension_semantics=("arbitrary",), vmem_limit_bytes=VMEM_LIMIT_BYTES),
        name="proj_in",
    )(x, w_bf16)


def _sb_prompt_kernel(bias_ref, q_ref, k_ref, v_ref, o_ref, *, tile):
    hp = pl.program_id(1)
    qi = pl.program_id(2)
    heads = q_ref.shape[-1] // SB_HEAD_DIM
    row = lax.broadcasted_iota(jnp.int32, (tile, tile), 0)
    col = lax.broadcasted_iota(jnp.int32, (tile, tile), 1)
    later = (row > col).astype(bf16)
    causal = col < row
    q_all = q_ref[...] * (SB_SCALE * LOG2_E)
    qh = [q_all[:, hh * SB_HEAD_DIM:(hh + 1) * SB_HEAD_DIM].astype(bf16) for hh in range(heads)]
    bias = [bias_ref[hp * heads + hh] * LOG2_E for hh in range(heads)]

    def key_tile(j, state, masked):
        start = pl.multiple_of(j * tile, tile)
        hs = range(heads)
        lanes = [slice(h * SB_HEAD_DIM, (h + 1) * SB_HEAD_DIM) for h in hs]
        kh = [k_ref[pl.ds(start, tile), lanes[h]].astype(bf16) for h in hs]
        z2 = [lax.dot_general(qh[h], kh[h], (((1,), (1,)), ((), ())), preferred_element_type=f32) + bias[h]
              for h in hs]
        sp2 = [jnp.maximum(z2[h], 0.0) + jnp.log2(1.0 + jnp.exp2(-jnp.abs(z2[h]))) for h in hs]
        drop = [jnp.where(causal, sp2[h], 0.0) if masked else sp2[h] for h in hs]
        inner = [jnp.dot(drop[h].astype(bf16), later, preferred_element_type=f32) for h in hs]
        a = [jnp.exp2(z2[h] - (sp2[h] + state[h][0] + inner[h])) for h in hs]
        if masked:
            a = [jnp.where(causal, a[h], 0.0) for h in hs]
        vh = [v_ref[pl.ds(start, tile), lanes[h]].astype(bf16) for h in hs]
        acc = [state[h][1] + jnp.dot(a[h].astype(bf16), vh[h], preferred_element_type=f32) for h in hs]
        carry = [state[h][0] + jnp.sum(drop[h], axis=1, keepdims=True) for h in hs]
        return tuple((carry[h], acc[h]) for h in hs)

    state = tuple((jnp.zeros((tile, 1), f32), jnp.zeros((tile, SB_HEAD_DIM), f32)) for _ in range(heads))
    state = key_tile(qi, state, True)
    state = lax.fori_loop(0, qi, lambda s, st: key_tile(qi - 1 - s, st, False), state)
    o_ref[...] = jnp.concatenate([acc for _, acc in state], axis=1)


def _sb_prompt(q, k, v, bias, tile=SB_TILE):
    b, s, w = q.shape
    assert s % tile == 0 and w % SB_STEP_WIDTH == 0
    blk = lambda bb, hp, qi: (bb, qi, hp)
    full = lambda bb, hp, qi: (bb, 0, hp)
    return pl.pallas_call(
        functools.partial(_sb_prompt_kernel, tile=tile),
        grid=(b, w // SB_STEP_WIDTH, s // tile),
        in_specs=[
            pl.BlockSpec(memory_space=pltpu.SMEM),
            pl.BlockSpec((None, tile, SB_STEP_WIDTH), blk),
            pl.BlockSpec((None, s, SB_STEP_WIDTH), full),
            pl.BlockSpec((None, s, SB_STEP_WIDTH), full),
        ],
        out_specs=pl.BlockSpec((None, tile, SB_STEP_WIDTH), blk),
        out_shape=jax.ShapeDtypeStruct((b, s, w), f32),
        compiler_params=pltpu.CompilerParams(
            dimension_semantics=("arbitrary", "arbitrary", "arbitrary"),
            vmem_limit_bytes=VMEM_LIMIT_BYTES),
        name="sb_prompt",
    )(bias, q, k, v)


SBS_PAGES_PER_STEP = 32


def _sb_sample_kernel(pt_ref, bias_ref, q_ref, *refs, n_slots):
    del pt_ref
    k_refs = refs[:n_slots]
    v_refs = refs[n_slots:2 * n_slots]
    o_ref = refs[2 * n_slots]
    qb_ref, acc_ref, carry_ref, z_ref = refs[2 * n_slots + 1:]
    step = pl.program_id(1)

    @pl.when(step == 0)
    def _():
        q = q_ref[...] * SB_SCALE
        qb_ref[...] = jnp.broadcast_to(q, qb_ref.shape)
        acc_ref[...] = jnp.zeros_like(acc_ref)
        carry_ref[...] = jnp.zeros_like(carry_ref)

    row = lax.broadcasted_iota(jnp.int32, (PAGE_SIZE, PAGE_SIZE), 0)
    col = lax.broadcasted_iota(jnp.int32, (PAGE_SIZE, PAGE_SIZE), 1)
    later = (row > col).astype(bf16)
    later2 = jnp.concatenate([later, later], axis=0)

    for i, k_ref in enumerate(k_refs):
        for h in range(SB_HEADS):
            zh = jnp.sum(k_ref[h] * qb_ref[h], axis=0, keepdims=True) + bias_ref[h]
            z_ref[pl.ds(i * SB_HEADS + h, 1), :] = zh
    z = z_ref[...]
    softplus = jnp.maximum(z, 0.0) + jnp.log(1.0 + jnp.exp(-jnp.abs(z)))
    log_keep = -softplus
    hi = log_keep.astype(bf16)
    lo = (log_keep - hi.astype(f32)).astype(bf16)
    inner = jnp.dot(jnp.concatenate([hi, lo], axis=1), later2, preferred_element_type=f32)
    total = jnp.sum(log_keep, axis=1, keepdims=True)
    base = z - softplus + inner
    carry = carry_ref[...]
    for i in range(n_slots):
        sl = slice(i * SB_HEADS, (i + 1) * SB_HEADS)
        w = jnp.exp(base[sl] + carry)
        for h in range(SB_HEADS):
            acc_ref[h] += v_refs[i][h] * w[h:h + 1, :]
        carry = carry + total[sl]
    carry_ref[...] = carry

    @pl.when(step == pl.num_programs(1) - 1)
    def _():
        o_ref[...] = jnp.sum(acc_ref[...], axis=2, keepdims=True)


def _sb_sample_past(q, bias, k_pool, v_pool, page_table):
    b, n_pages = page_table.shape
    n_slots = SBS_PAGES_PER_STEP
    assert n_pages % n_slots == 0
    page_block = (None, SB_HEADS, SB_HEAD_DIM, PAGE_SIZE)

    def page_spec(slot):
        return pl.BlockSpec(
            page_block, lambda bb, s, pt, slot=slot: (pt[bb, n_pages - 1 - (s * n_slots + slot)], 0, 0, 0))

    grid_spec = pltpu.PrefetchScalarGridSpec(
        num_scalar_prefetch=1,
        grid=(b, n_pages // n_slots),
        in_specs=[
            pl.BlockSpec(memory_space=pltpu.SMEM),
            pl.BlockSpec((None, SB_HEADS, SB_HEAD_DIM, 1), lambda bb, s, pt: (bb, 0, 0, 0)),
        ] + [page_spec(i) for i in range(n_slots)] * 2,
        out_specs=pl.BlockSpec((None, SB_HEADS, SB_HEAD_DIM, 1), lambda bb, s, pt: (bb, 0, 0, 0)),
        scratch_shapes=[
            pltpu.VMEM((SB_HEADS, SB_HEAD_DIM, PAGE_SIZE), f32),
            pltpu.VMEM((SB_HEADS, SB_HEAD_DIM, PAGE_SIZE), f32),
            pltpu.VMEM((SB_HEADS, 1), f32),
            pltpu.VMEM((n_slots * SB_HEADS, PAGE_SIZE), f32),
        ],
    )
    o = pl.pallas_call(
        functools.partial(_sb_sample_kernel, n_slots=n_slots),
        grid_spec=grid_spec,
        out_shape=jax.ShapeDtypeStruct((b, SB_HEADS, SB_HEAD_DIM, 1), f32),
        compiler_params=pltpu.CompilerParams(
            dimension_semantics=("arbitrary", "arbitrary"), vmem_limit_bytes=VMEM_LIMIT_BYTES),
        name="sb_sample",
    )(page_table, bias, q[..., None], *([k_pool] * n_slots), *([v_pool] * n_slots))
    return o[..., 0]


MAX_TOKEN_TILE = 512


def _token_tile(t):
    return max(m for m in range(SUBLANES, MAX_TOKEN_TILE + 1, SUBLANES) if t % m == 0)


def _route_kernel(h_ref, whi_ref, wlo_ref, b_ref, e_ref, p_ref, cnt_ref, hb_ref):
    h = h_ref[...]
    h_hi = h.astype(bf16)
    hb_ref[...] = h_hi
    h_lo = (h - h_hi.astype(f32)).astype(bf16)
    dot = lambda a, w_ref: jnp.dot(a, w_ref[...], preferred_element_type=f32)
    work = dot(h_hi, whi_ref) + (dot(h_hi, wlo_ref) + dot(h_lo, whi_ref)) + b_ref[...]
    lane = lax.broadcasted_iota(jnp.int32, work.shape, 1)
    picked = jnp.zeros(work.shape, jnp.bool_)
    top_e, top_l = [], []
    for _ in range(TOP_K):
        m = jnp.max(work, axis=1, keepdims=True)
        idx = jnp.min(jnp.where(work == m, lane, N_EXPERTS), axis=1, keepdims=True)
        hit = lane == idx
        picked = picked | hit
        work = jnp.where(hit, -jnp.inf, work)
        top_e.append(idx)
        top_l.append(m)
    ex = [jnp.exp(l - top_l[0]) for l in top_l]
    den = ex[0] + ex[1] + ex[2] + ex[3]
    e_ref[...] = jnp.concatenate(top_e, axis=1)
    p_ref[...] = jnp.concatenate([x / den for x in ex], axis=1)
    cnt_ref[...] = jnp.sum(picked.astype(f32), axis=0, keepdims=True).astype(jnp.int32)


def _route(h, w_router, b_router):
    t, d = h.shape
    tile = _token_tile(t)
    w_hi = w_router.astype(bf16)
    w_lo = (w_router - w_hi.astype(f32)).astype(bf16)
    const = lambda a: pl.BlockSpec(a.shape, lambda i: (0,) * a.ndim)
    b2 = b_router[None, :]
    return pl.pallas_call(
        _route_kernel,
        grid=(t // tile,),
        in_specs=[pl.BlockSpec((tile, d), lambda i: (i, 0)), const(w_hi), const(w_lo), const(b2)],
        out_specs=[pl.BlockSpec((tile, TOP_K), lambda i: (i, 0)), pl.BlockSpec((tile, TOP_K), lambda i: (i, 0)),
                   pl.BlockSpec((None, 1, N_EXPERTS), lambda i: (i, 0, 0)), pl.BlockSpec((tile, d), lambda i: (i, 0))],
        out_shape=[jax.ShapeDtypeStruct((t, TOP_K), jnp.int32), jax.ShapeDtypeStruct((t, TOP_K), f32),
                   jax.ShapeDtypeStruct((t // tile, 1, N_EXPERTS), jnp.int32), jax.ShapeDtypeStruct((t, d), bf16)],
        compiler_params=pltpu.CompilerParams(
            dimension_semantics=("arbitrary",), vmem_limit_bytes=VMEM_LIMIT_BYTES),
        name="moe_route",
    )(h, w_hi, w_lo, b2)


def _pair_pos_kernel(e_ref, base_ref, pos_ref):
    tile = e_ref.shape[0]
    lane = lax.broadcasted_iota(jnp.int32, (tile, N_EXPERTS), 1)
    hits = [lane == e_ref[:, j:j + 1] for j in range(TOP_K)]
    any_hit = hits[0] | hits[1] | hits[2] | hits[3]
    row = lax.broadcasted_iota(jnp.int32, (tile, tile), 0)
    col = lax.broadcasted_iota(jnp.int32, (tile, tile), 1)
    earlier = (col < row).astype(bf16)
    before = jnp.dot(earlier, any_hit.astype(bf16), preferred_element_type=f32) + base_ref[...].astype(f32)
    pos = [jnp.sum(jnp.where(hit, before, 0.0), axis=1, keepdims=True) for hit in hits]
    pos_ref[...] = jnp.concatenate(pos, axis=1).astype(jnp.int32)


def _pair_pos(top_e, tile_base):
    t = top_e.shape[0]
    n_tiles = tile_base.shape[0]
    tile = t // n_tiles
    return pl.pallas_call(
        _pair_pos_kernel,
        grid=(n_tiles,),
        in_specs=[pl.BlockSpec((tile, TOP_K), lambda i: (i, 0)),
                  pl.BlockSpec((None, 1, N_EXPERTS), lambda i: (i, 0, 0))],
        out_specs=pl.BlockSpec((tile, TOP_K), lambda i: (i, 0)),
        out_shape=jax.ShapeDtypeStruct((t, TOP_K), jnp.int32),
        compiler_params=pltpu.CompilerParams(
            dimension_semantics=("arbitrary",), vmem_limit_bytes=VMEM_LIMIT_BYTES),
        name="moe_pair_pos",
    )(top_e, tile_base)


def _moe_kernel(tile_ref, exp_ref, lo_ref, hi_ref, first_ref, x_ref, wgu_ref, bg_ref, bl_ref, wd_ref, bd_ref,
                o_ref, wg_s, wl_s, wd_s):
    del tile_ref
    i = pl.program_id(0)
    ff = wd_ref.shape[0]
    half = MXU_DIM // 2

    src = lax.broadcasted_iota(jnp.int32, (MXU_DIM, MXU_DIM), 0)
    dst = lax.broadcasted_iota(jnp.int32, (MXU_DIM, MXU_DIM), 1)
    unzip = (src == jnp.where(dst < half, 2 * dst, 2 * (dst - half) + 1)).astype(bf16)

    @pl.when((i == 0) | (exp_ref[i] != exp_ref[jnp.maximum(i - 1, 0)]))
    def _():
        for c in range(2 * ff // MXU_DIM):
            cols = slice(c * MXU_DIM, (c + 1) * MXU_DIM)
            out = slice(c * half, (c + 1) * half)
            w = jnp.dot(wgu_ref[:, cols].astype(bf16), unzip, preferred_element_type=f32).astype(bf16)
            wg_s[:, out] = w[:, :half]
            wl_s[:, out] = w[:, half:]
        wd_s[...] = wd_ref[...].astype(bf16)

    @pl.when(first_ref[i] == 1)
    def _():
        o_ref[...] = jnp.zeros_like(o_ref)

    @pl.when(hi_ref[i] > lo_ref[i])
    def _():
        x = x_ref[...]
        g = jnp.dot(x, wg_s[...], preferred_element_type=f32) + bg_ref[...]
        l = jnp.dot(x, wl_s[...], preferred_element_type=f32) + bl_ref[...]
        glu = jnp.minimum(g, SWIGLU_LIMIT)
        lin = jnp.clip(l, -SWIGLU_LIMIT, SWIGLU_LIMIT)
        h = glu * (1.0 / (1.0 + jnp.exp(-SWIGLU_ALPHA * glu))) * (lin + 1.0)
        y = jnp.dot(h.astype(bf16), wd_s[...], preferred_element_type=f32) + bd_ref[...]
        row = lax.broadcasted_iota(jnp.int32, (x.shape[0], 1), 0)
        mine = (row >= lo_ref[i]) & (row < hi_ref[i])
        o_ref[...] = jnp.where(mine, y.astype(o_ref.dtype), o_ref[...])


def _moe_experts(xs, items, w_gate_up, b_gate_up, w_down, b_down, tile_m):
    n_rows, d = xs.shape
    ff = w_down.shape[1]
    n_items = items[0].shape[0]
    row = lambda i, tl, ex, lo, hi, fi: (tl[i], 0)
    wsel = lambda i, tl, ex, lo, hi, fi: (ex[i], 0, 0)
    grid_spec = pltpu.PrefetchScalarGridSpec(
        num_scalar_prefetch=5,
        grid=(n_items,),
        in_specs=[
            pl.BlockSpec((tile_m, d), row),
            pl.BlockSpec((None, d, 2 * ff), wsel),
            pl.BlockSpec((None, 1, ff), wsel),
            pl.BlockSpec((None, 1, ff), wsel),
            pl.BlockSpec((None, ff, d), wsel),
            pl.BlockSpec((None, 1, d), wsel),
        ],
        out_specs=pl.BlockSpec((tile_m, d), row),
        scratch_shapes=[
            pltpu.VMEM((d, ff), bf16), pltpu.VMEM((d, ff), bf16), pltpu.VMEM((ff, d), bf16),
        ],
    )
    return pl.pallas_call(
        _moe_kernel,
        grid_spec=grid_spec,
        out_shape=jax.ShapeDtypeStruct((n_rows, d), bf16),
        compiler_params=pltpu.CompilerParams(
            dimension_semantics=("arbitrary",), vmem_limit_bytes=VMEM_LIMIT_BYTES),
        name="moe_experts",
    )(*items, xs, w_gate_up, b_gate_up[:, None, 0::2], b_gate_up[:, None, 1::2], w_down, b_down[:, None, :])


def _moe_work_items(starts, n_tiles, tile_m):
    counts = starts[1:] - starts[:-1]
    first_tile = starts[:-1] // tile_m
    n_t = jnp.where(counts > 0, (starts[1:] - 1) // tile_m - first_tile + 1, 0)
    item_end = jnp.cumsum(n_t)
    item_start = item_end - n_t
    n_items = n_tiles + N_EXPERTS - 1
    i = jnp.arange(n_items, dtype=jnp.int32)
    valid = i < item_end[-1]
    ii = jnp.minimum(i, item_end[-1] - 1)
    exp = jnp.sum((ii[:, None] >= item_end[None, :]).astype(jnp.int32), axis=1)
    onehot = (exp[:, None] == jnp.arange(N_EXPERTS, dtype=jnp.int32)[None, :]).astype(jnp.int32)
    pick = lambda table: jnp.sum(onehot * table[None, :], axis=1)
    tile = pick(first_tile) + ii - pick(item_start)
    lo = jnp.clip(pick(starts[:-1]) - tile * tile_m, 0, tile_m)
    hi = jnp.clip(pick(starts[1:]) - tile * tile_m, 0, tile_m)
    hi = jnp.where(valid, hi, lo)
    first = jnp.concatenate([jnp.ones((1,), jnp.int32), (tile[1:] != tile[:-1]).astype(jnp.int32)])
    return tuple(a.astype(jnp.int32) for a in (tile, exp, lo, hi, first))


def _moe(h, w_router, b_router, w_gate_up, b_gate_up, w_down, b_down, tile_m=MOE_TILE_M):
    t = h.shape[0]
    n_pairs = t * TOP_K
    n_tiles = -(-n_pairs // tile_m)
    top_e, top_p, tile_counts, h_bf16 = _route(h, w_router, b_router)
    counts = jnp.sum(tile_counts[:, 0, :], axis=0)
    starts = jnp.concatenate([jnp.zeros((1,), jnp.int32), jnp.cumsum(counts)]).astype(jnp.int32)
    tile_base = (jnp.cumsum(tile_counts, axis=0) - tile_counts + starts[None, None, :-1]).astype(jnp.int32)
    pos = _pair_pos(top_e, tile_base)
    bits = max(1, (n_pairs - 1).bit_length())
    assert N_EXPERTS << bits < 2 ** 31
    keys = (top_e.reshape(-1) << bits) | jnp.arange(n_pairs, dtype=jnp.int32)
    order = jnp.sort(keys) & ((1 << bits) - 1)
    row_tok = jnp.concatenate([order // TOP_K, jnp.zeros((n_tiles * tile_m - n_pairs,), jnp.int32)])
    tall = jnp.concatenate([h_bf16, jnp.zeros((n_tiles * tile_m - t + 8, h.shape[1]), bf16)], axis=0)
    tall, idx = lax.optimization_barrier((tall, row_tok))
    xs = lax.optimization_barrier(tall[idx])
    ys = _moe_experts(xs, _moe_work_items(starts, n_tiles, tile_m), w_gate_up, b_gate_up, w_down, b_down, tile_m)
    picked = [lax.optimization_barrier(ys[pos[:, j]]) for j in range(TOP_K)]
    return picked, top_p


RW_LANE_HALF = LANES // 2
RW_KJ = RW_HEAD_DIM // 2
RW_VG = RW_HEAD_DIM // 8
RW_STEPS_PER_BLOCK = 32


def _rwkv_scan_kernel(kk_ref, d_ref, kka_ref, km_ref, r_ref, v_ref, s0_ref, y_ref, s_ref, *, steps):
    tb = pl.program_id(1)

    @pl.when(tb == 0)
    def _():
        s_ref[...] = s0_ref[...]

    def fold(x):
        return x + pltpu.roll(x, RW_LANE_HALF, axis=1)

    def project(kk_rows):
        acc = [None] * RW_VG
        for j in range(RW_KJ):
            kkj = kk_rows(j)
            for g in range(RW_VG):
                p = s_ref[g, j] * kkj
                acc[g] = p if acc[g] is None else acc[g] + p
        return tuple(acc)

    def step(t, acc, t_next):
        sa = [-fold(a) for a in acc]
        vv = [v_ref[t, g] for g in range(RW_VG)]
        yacc = [None] * RW_VG
        nxt = [None] * RW_VG
        for j in range(RW_KJ):
            dj = d_ref[t, pl.ds(j, 1), :]
            kkaj = kka_ref[t, pl.ds(j, 1), :]
            kmj = km_ref[t, pl.ds(j, 1), :]
            rj = r_ref[t, pl.ds(j, 1), :]
            kkn = None if t_next is None else kk_ref[t_next, pl.ds(j, 1), :]
            for g in range(RW_VG):
                s = s_ref[g, j] * dj + sa[g] * kkaj + vv[g] * kmj
                s_ref[g, j] = s
                p = s * rj
                yacc[g] = p if yacc[g] is None else yacc[g] + p
                if kkn is not None:
                    q = s * kkn
                    nxt[g] = q if nxt[g] is None else nxt[g] + q
        for g in range(RW_VG):
            y_ref[t, g] = fold(yacc[g])
        return tuple(nxt)

    acc = project(lambda j: kk_ref[0, pl.ds(j, 1), :])
    acc = lax.fori_loop(0, steps - 1, lambda t, a: step(t, a, t + 1), acc)
    step(steps - 1, acc, None)


def _rwkv_scan(kk, d, kka, km, r, v, s0):
    n_g, s_len = kk.shape[:2]
    steps = min(RW_STEPS_PER_BLOCK, s_len)
    assert s_len % steps == 0
    krow = pl.BlockSpec((None, steps, RW_KJ, LANES), lambda g, t: (g, t, 0, 0))
    vrow = pl.BlockSpec((None, steps, RW_VG, 8, LANES), lambda g, t: (g, t, 0, 0, 0))
    st = pl.BlockSpec((None, RW_VG, RW_KJ, 8, LANES), lambda g, t: (g, 0, 0, 0, 0))
    return pl.pallas_call(
        functools.partial(_rwkv_scan_kernel, steps=steps),
        grid=(n_g, s_len // steps),
        in_specs=[krow, krow, krow, krow, krow, vrow, st],
        out_specs=[vrow, st],
        out_shape=[jax.ShapeDtypeStruct(v.shape, f32), jax.ShapeDtypeStruct(s0.shape, f32)],
        compiler_params=pltpu.CompilerParams(
            dimension_semantics=("arbitrary", "arbitrary"), vmem_limit_bytes=VMEM_LIMIT_BYTES),
        name="rwkv_scan",
    )(kk, d, kka, km, r, v, s0)


def _rw_groups(n_bh):
    assert n_bh % RW_LANE_HALF == 0
    return n_bh // RW_LANE_HALF


def _rw_k_layout(x):
    b, s, h, _ = x.shape
    g = _rw_groups(b * h)
    x = x.reshape(b, s, h, 2, RW_KJ)
    x = jnp.transpose(x, (1, 4, 3, 0, 2)).reshape(s, RW_KJ, 2, g, RW_LANE_HALF)
    return jnp.transpose(x, (3, 0, 1, 2, 4)).reshape(g, s, RW_KJ, LANES)


def _rw_v_layout(x):
    b, s, h, _ = x.shape
    g = _rw_groups(b * h)
    x = jnp.transpose(x, (1, 3, 0, 2)).reshape(s, RW_HEAD_DIM, g, RW_LANE_HALF)
    x = jnp.transpose(x, (2, 0, 1, 3))
    x = jnp.concatenate([x, x], axis=-1)
    return x.reshape(g, s, RW_VG, 8, LANES)


def _rw_v_unlayout(y, b, h):
    g, s = y.shape[:2]
    y = y.reshape(g, s, RW_HEAD_DIM, LANES)[..., :RW_LANE_HALF]
    y = jnp.transpose(y, (1, 2, 0, 3)).reshape(s, RW_HEAD_DIM, b, h)
    return jnp.transpose(y, (2, 0, 3, 1))


def _rw_state_layout(wkv):
    b, h = wkv.shape[:2]
    g = _rw_groups(b * h)
    x = wkv.reshape(g, RW_LANE_HALF, RW_VG, 8, 2, RW_KJ)
    return jnp.transpose(x, (0, 2, 5, 3, 4, 1)).reshape(g, RW_VG, RW_KJ, 8, LANES)


def _rw_state_unlayout(st, b, h):
    g = st.shape[0]
    x = st.reshape(g, RW_VG, RW_KJ, 8, 2, RW_LANE_HALF)
    x = jnp.transpose(x, (0, 5, 1, 3, 4, 2))
    return x.reshape(b, h, RW_HEAD_DIM, RW_HEAD_DIM)


MIX_TILE_M = 512


def _layer_norm_rows(pre, g, b):
    mu = jnp.mean(pre, axis=1, keepdims=True)
    cen = pre - mu
    var = jnp.mean(cen * cen, axis=1, keepdims=True)
    return cen * lax.rsqrt(var + LN_EPS) * g + b


def _sigmoid(x):
    return 1.0 / (1.0 + jnp.exp(-x))


def _mem_attend(q, mk_ref, mv_ref):
    heads = []
    for hd in range(MEM_HEADS):
        lanes = slice(hd * MEM_HEAD_DIM, (hd + 1) * MEM_HEAD_DIM)
        s = lax.dot_general(q[:, lanes], mk_ref[:, lanes].astype(bf16), (((1,), (1,)), ((), ())),
                            preferred_element_type=f32) * MEM_SCALE
        p = jnp.exp(s - jnp.max(s, axis=1, keepdims=True))
        p = p / jnp.sum(p, axis=1, keepdims=True)
        heads.append(jnp.dot(p.astype(bf16), mv_ref[:, lanes].astype(bf16), preferred_element_type=f32))
    return jnp.concatenate(heads, axis=1)


def _mem_attend_kernel(q_ref, mk_ref, mv_ref, o_ref):
    o_ref[...] = _mem_attend(q_ref[...].astype(bf16), mk_ref, mv_ref)


def _mem_attend_tokens(q_mem, mem_k, mem_v):
    b, s, w = q_mem.shape
    tok = pl.BlockSpec((None, s, w), lambda bb: (bb, 0, 0))
    mem = pl.BlockSpec((None, N_MEM, MEM_WIDTH), lambda bb: (bb, 0, 0))
    return pl.pallas_call(
        _mem_attend_kernel,
        grid=(b,),
        in_specs=[tok, mem, mem],
        out_specs=tok,
        out_shape=jax.ShapeDtypeStruct((b, s, w), f32),
        compiler_params=pltpu.CompilerParams(
            dimension_semantics=("arbitrary",), vmem_limit_bytes=VMEM_LIMIT_BYTES),
        name="mem_attend",
    )(q_mem, mem_k, mem_v)


def _mix_kernel(*refs, attend):
    if attend:
        x_ref, osb_ref, third_ref, orw_ref, gsb_ref, gmem_ref, grw_ref, mk_ref, mv_ref = refs[:9]
        o_mem = _mem_attend(third_ref[...].astype(bf16), mk_ref, mv_ref)
    else:
        x_ref, osb_ref, third_ref, orw_ref, gsb_ref, gmem_ref, grw_ref = refs[:7]
        o_mem = third_ref[...]
    wsb_ref, wmem_ref, wrw_ref, wout_ref, g1_ref, b1_ref, h_ref = refs[-7:]
    branch = lambda o, w_ref: jnp.dot(o.astype(bf16), w_ref[...], preferred_element_type=f32)
    merged = (_sigmoid(gsb_ref[...]) * branch(osb_ref[...], wsb_ref)
              + _sigmoid(gmem_ref[...]) * branch(o_mem, wmem_ref)
              + _sigmoid(grw_ref[...]) * branch(orw_ref[...], wrw_ref))
    pre = DN_ALPHA * x_ref[...] + branch(merged, wout_ref)
    h_ref[...] = _layer_norm_rows(pre, g1_ref[...], b1_ref[...])


def _mix(x, o_sb, q_or_o_mem, o_rw, g_sb, g_mem, g_rw, mem_kv, w_sb_o, w_mem_o, w_rw_o, w_out, ln_g, ln_b):
    b, s, d = x.shape
    tile = min(MIX_TILE_M, s)
    assert s % tile == 0
    tok = lambda w: pl.BlockSpec((None, tile, w), lambda bb, i: (bb, i, 0))
    mem = pl.BlockSpec((None, N_MEM, MEM_WIDTH), lambda bb, i: (bb, 0, 0))
    const = lambda a: pl.BlockSpec(a.shape, lambda bb, i: (0,) * a.ndim)
    weights = [w.astype(bf16) for w in (w_sb_o, w_mem_o, w_rw_o, w_out)] + [ln_g[None, :], ln_b[None, :]]
    mems = () if mem_kv is None else tuple(mem_kv)
    return pl.pallas_call(
        functools.partial(_mix_kernel, attend=mem_kv is not None),
        grid=(b, s // tile),
        in_specs=[tok(d), tok(SB_WIDTH), tok(MEM_WIDTH), tok(RW_WIDTH), tok(d), tok(d), tok(d)]
        + [mem] * len(mems) + [const(w) for w in weights],
        out_specs=tok(d),
        out_shape=jax.ShapeDtypeStruct((b, s, d), f32),
        compiler_params=pltpu.CompilerParams(
            dimension_semantics=("arbitrary", "arbitrary"), vmem_limit_bytes=VMEM_LIMIT_BYTES),
        name="mix_ln1",
    )(x, o_sb, q_or_o_mem, o_rw, g_sb, g_mem, g_rw, *mems, *weights)


def _combine_kernel(*refs):
    pick_refs = refs[:TOP_K]
    p_ref, h_ref, g_ref, b_ref, y_ref = refs[TOP_K:]
    p = p_ref[...]
    moe = pick_refs[0][...].astype(f32) * p[:, 0:1]
    for j in range(1, TOP_K):
        moe = moe + pick_refs[j][...].astype(f32) * p[:, j:j + 1]
    y_ref[...] = _layer_norm_rows(DN_ALPHA * h_ref[...] + moe, g_ref[...], b_ref[...])


def _combine(picked, top_p, h, ln_g, ln_b, first_row, n_rows):
    d = h.shape[1]
    tile = min(MIX_TILE_M, n_rows)
    assert n_rows % tile == 0 and first_row % tile == 0
    first = first_row // tile
    row = lambda w: pl.BlockSpec((tile, w), lambda i: (first + i, 0))
    vec = pl.BlockSpec((1, d), lambda i: (0, 0))
    return pl.pallas_call(
        _combine_kernel,
        grid=(n_rows // tile,),
        in_specs=[row(d)] * TOP_K + [row(TOP_K), row(d), vec, vec],
        out_specs=pl.BlockSpec((tile, d), lambda i: (i, 0)),
        out_shape=jax.ShapeDtypeStruct((n_rows, d), f32),
        compiler_params=pltpu.CompilerParams(
            dimension_semantics=("arbitrary",), vmem_limit_bytes=VMEM_LIMIT_BYTES),
        name="combine_ln2",
    )(*picked, top_p, h, ln_g[None, :], ln_b[None, :])


def _sb_logits(q, k, bias):
    z = jnp.einsum('bqhd,bkhd->bhqk', q, k, preferred_element_type=f32) * SB_SCALE
    return z + bias.astype(f32)[None, :, None, None]


def _sb_weights(z, causal):
    log_keep = jnp.where(causal, jax.nn.log_sigmoid(-z), 0.0)
    log_survive = lax.cumsum(log_keep, axis=z.ndim - 1, reverse=True) - log_keep
    return jnp.where(causal, jnp.exp(jax.nn.log_sigmoid(z) + log_survive), 0.0)


def _sb_sample(q, k_new, v_new, bias, k_pool, v_pool, page_table):
    b, s = q.shape[:2]
    assert s == 1
    to_token_minor = lambda pool: jnp.transpose(pool, (0, 2, 3, 1))
    o_past = _sb_sample_past(q[:, 0], bias, to_token_minor(k_pool), to_token_minor(v_pool), page_table)
    new_pos = jnp.arange(s)
    w_new = _sb_weights(_sb_logits(q, k_new, bias), new_pos[None, :] < new_pos[:, None])
    o_new = jnp.einsum('bhqk,bkhd->bqhd', w_new, v_new)
    return (o_past[:, None] + o_new).reshape(b, s, SB_WIDTH)


def _rwkv7(p_rw, shift0, wkv0, mu_rw, w_decay0, w_decay2, w_aaa0, w_aaa2, w_gate2,
           rw_k_k, rw_k_a, rw_r_k, rw_gn_g, rw_gn_b):
    b, s, _ = p_rw.shape
    prev = jnp.concatenate([shift0[:, None, :], p_rw[:, :-1]], axis=1)
    xs = p_rw + mu_rw * (prev - p_rw)
    r, k, v, w_lo, a_lo, g_lo = _split(xs, RW_SPLITS)
    w_raw = w_decay0 + jnp.tanh(w_lo) @ w_decay2
    decay = jnp.exp(-jnp.exp(-jax.nn.softplus(-w_raw) - 0.5))
    a = jax.nn.sigmoid(w_aaa0 + a_lo @ w_aaa2)
    g = jax.nn.sigmoid(g_lo) @ w_gate2
    heads = lambda t: t.reshape(b, s, RW_HEADS, RW_HEAD_DIM)
    kk = heads(k * rw_k_k)
    kk = kk / jnp.maximum(jnp.sqrt(jnp.sum(kk * kk, axis=-1, keepdims=True)), 1e-12)
    k_mod = k * (1.0 + (a - 1.0) * rw_k_a)
    rh, kh, vh, ah, dh = heads(r), heads(k_mod), heads(v), heads(a), heads(decay)

    y, wkv = _rwkv_scan(_rw_k_layout(kk), _rw_k_layout(dh), _rw_k_layout(kk * ah), _rw_k_layout(kh),
                        _rw_k_layout(rh), _rw_v_layout(vh), _rw_state_layout(wkv0))
    y = _rw_v_unlayout(y, b, RW_HEADS)
    wkv = _rw_state_unlayout(wkv, b, RW_HEADS)
    mu = y.mean(-1, keepdims=True)
    var = jnp.square(y - mu).mean(-1, keepdims=True)
    y = ((y - mu) * lax.rsqrt(var + RW_GN_EPS)).reshape(b, s, RW_WIDTH) * rw_gn_g + rw_gn_b
    bonus = jnp.sum(rh * kh * rw_r_k, axis=-1, keepdims=True) * vh
    y = (y + bonus.reshape(b, s, RW_WIDTH)) * g
    return y, p_rw[:, -1], wkv


def _branch_mix(x, parts, o_sb, mem_k, mem_v, shift0, wkv0, lw):
    q_mem, p_rw, g_sb, g_mem, g_rw = parts
    o_rw, shift, wkv = _rwkv7(p_rw, shift0, wkv0, lw['mu_rw'], lw['w_decay0'], lw['w_decay2'],
                              lw['w_aaa0'], lw['w_aaa2'], lw['w_gate2'], lw['rw_k_k'], lw['rw_k_a'],
                              lw['rw_r_k'], lw['rw_gn_g'], lw['rw_gn_b'])
    tail = (lw['w_sb_o'], lw['w_mem_o'], lw['w_rw_o'], lw['w_out'], lw['ln1_g'], lw['ln1_b'])
    if x.shape[1] == 1:
        rows = lambda a: a.reshape(1, a.shape[0], a.shape[2])
        o_mem = _mem_attend_tokens(q_mem, mem_k, mem_v)
        h = _mix(rows(x), rows(o_sb), rows(o_mem), rows(o_rw), rows(g_sb), rows(g_mem), rows(g_rw), None, *tail)
        h = h.reshape(x.shape)
    else:
        h = _mix(x, o_sb, q_mem, o_rw, g_sb, g_mem, g_rw, (mem_k, mem_v), *tail)
    return h, shift, wkv


def kernel(x_prompt, x_sample, cache_sb_k, cache_sb_v, cache_mem_k, cache_mem_v, state_rw_shift,
           state_rw_wkv, page_table, mem_prompt, w_in, sb_bias, mu_rw, w_decay0, w_decay2, w_aaa0,
           w_aaa2, w_gate2, rw_k_k, rw_k_a, rw_r_k, rw_gn_g, rw_gn_b, w_mem_kv, w_sb_o, w_mem_o,
           w_rw_o, w_out, ln1_g, ln1_b, w_router, b_router, w_gate_up, b_gate_up, w_down, b_down,
           ln2_g, ln2_b):
    assert w_in.shape[0] == DEPTH == 1
    l = 0
    lw = dict(mu_rw=mu_rw[l], w_decay0=w_decay0[l], w_decay2=w_decay2[l], w_aaa0=w_aaa0[l],
              w_aaa2=w_aaa2[l], w_gate2=w_gate2[l], rw_k_k=rw_k_k[l], rw_k_a=rw_k_a[l],
              rw_r_k=rw_r_k[l], rw_gn_g=rw_gn_g[l], rw_gn_b=rw_gn_b[l], w_sb_o=w_sb_o[l],
              w_mem_o=w_mem_o[l], w_rw_o=w_rw_o[l], w_out=w_out[l], ln1_g=ln1_g[l], ln1_b=ln1_b[l])
    bp, sp, _ = x_prompt.shape
    bs, ss, _ = x_sample.shape
    w_in_b = w_in[l].astype(bf16)

    xp = x_prompt.reshape(bp * sp, D_MODEL)
    q_sb, k_sb, v_sb, *rest = _proj(xp, w_in_b, IN_SPLITS, PROJ_TILE_M)
    tok3 = lambda a: a.reshape(bp, sp, a.shape[-1])
    q_sb, k_sb, v_sb = tok3(q_sb), tok3(k_sb), tok3(v_sb)
    o_sb = _sb_prompt(q_sb, k_sb, v_sb, sb_bias[l])
    mem_k, mem_v = _proj(mem_prompt.reshape(bp * N_MEM, D_MODEL), w_mem_kv[l].astype(bf16),
                         (MEM_WIDTH, MEM_WIDTH), PROJ_TILE_M)
    mem_k = mem_k.reshape(bp, N_MEM, MEM_WIDTH)
    mem_v = mem_v.reshape(bp, N_MEM, MEM_WIDTH)
    h_p, shift_p, wkv_p = _branch_mix(
        x_prompt, [tok3(a) for a in rest], o_sb, mem_k, mem_v, jnp.zeros((bp, RW_SHIFT_WIDTH), f32),
        jnp.zeros((bp, RW_HEADS, RW_HEAD_DIM, RW_HEAD_DIM), f32), lw)

    xs = x_sample.reshape(bs * ss, D_MODEL)
    q_s, k_s, v_s, *rest_s = _proj(xs, w_in_b, IN_SPLITS, bs * ss)
    tok3s = lambda a: a.reshape(bs, ss, a.shape[-1])
    heads_s = lambda a: a.reshape(bs, ss, SB_HEADS, SB_HEAD_DIM)
    o_sb_s = _sb_sample(heads_s(q_s), heads_s(k_s), heads_s(v_s), sb_bias[l], cache_sb_k[l], cache_sb_v[l],
                        page_table)
    h_s, shift_s, wkv_s = _branch_mix(
        x_sample, [tok3s(a) for a in rest_s], o_sb_s, cache_mem_k[l].reshape(bs, N_MEM, MEM_WIDTH),
        cache_mem_v[l].reshape(bs, N_MEM, MEM_WIDTH), state_rw_shift[l], state_rw_wkv[l], lw)

    h_all = jnp.concatenate([h_p.reshape(bp * sp, D_MODEL), h_s.reshape(bs * ss, D_MODEL)], axis=0)
    picked, top_p = _moe(h_all, w_router[l], b_router[l], w_gate_up[l], b_gate_up[l], w_down[l], b_down[l])
    y_prompt = _combine(picked, top_p, h_all, ln2_g[l], ln2_b[l], 0, bp * sp).reshape(bp, sp, D_MODEL)
    y_sample = _combine(picked, top_p, h_all, ln2_g[l], ln2_b[l], bp * sp, bs * ss).reshape(bs, ss, D_MODEL)

    heads_m = lambda a: a.reshape(1, bp, N_MEM, MEM_HEADS, MEM_HEAD_DIM)
    return (y_prompt, y_sample,
            k_sb.reshape(1, bp, sp, SB_HEADS, SB_HEAD_DIM), v_sb.reshape(1, bp, sp, SB_HEADS, SB_HEAD_DIM),
            heads_m(mem_k), heads_m(mem_v), shift_p[None], wkv_p[None],
            k_s.reshape(1, bs, ss, SB_HEADS, SB_HEAD_DIM), v_s.reshape(1, bs, ss, SB_HEADS, SB_HEAD_DIM),
            shift_s[None], wkv_s[None])
```

```python
import functools

import jax
import jax.numpy as jnp
import numpy as np
from jax import lax
from jax.experimental import pallas as pl
from jax.experimental.pallas import tpu as pltpu

D_MODEL = 1024
PAGE_SIZE = 128
N_MEM = 256
SB_HEADS = 8
SB_HEAD_DIM = 64
SB_WIDTH = SB_HEADS * SB_HEAD_DIM
SB_SCALE = SB_HEAD_DIM ** -0.5
LOG2_E = 1.4426950408889634
MEM_HEADS = 4
MEM_HEAD_DIM = 128
MEM_WIDTH = MEM_HEADS * MEM_HEAD_DIM
MEM_SCALE = MEM_HEAD_DIM ** -0.5
RW_HEADS = 8
RW_HEAD_DIM = 64
RW_WIDTH = RW_HEADS * RW_HEAD_DIM
DECAY_LORA = 64
AAA_LORA = 64
GATE_LORA = 128
RW_SPLITS = (RW_WIDTH, RW_WIDTH, RW_WIDTH, DECAY_LORA, AAA_LORA, GATE_LORA)
RW_SHIFT_WIDTH = sum(RW_SPLITS)
RW_GN_EPS = 64e-5
N_EXPERTS = 32
TOP_K = 4
D_FF = D_MODEL
SWIGLU_ALPHA = 1.702
SWIGLU_LIMIT = 7.0
DEPTH = 1
DN_ALPHA = (2 * DEPTH) ** 0.25
LN_EPS = 1e-5
IN_SPLITS = (SB_WIDTH, SB_WIDTH, SB_WIDTH, MEM_WIDTH, RW_SHIFT_WIDTH, D_MODEL, D_MODEL, D_MODEL)
IN_WIDTH = sum(IN_SPLITS)

LANES = 128
SUBLANES = 8
MXU_DIM = 256
VMEM_LIMIT_BYTES = 48 * 1024 * 1024

SB_TILE = MXU_DIM
SB_STEP_WIDTH = 8 * SB_HEAD_DIM
PROJ_TILE_M = 256
MOE_TILE_M = 256

f32 = jnp.float32
bf16 = jnp.bfloat16


def _split(x, sizes):
    return jnp.split(x, np.cumsum(sizes)[:-1].tolist(), axis=-1)


def _proj_kernel(x_ref, w_ref, *o_refs, splits, copies, transposed):
    xb = x_ref[...].astype(bf16)
    off = 0
    segs = []
    for o_ref, n in zip(o_refs, splits):
        seg = jnp.dot(xb, w_ref[:, off:off + n], preferred_element_type=f32)
        o_ref[...] = seg
        segs.append(seg)
        off += n
    n_c = len(copies)
    for c_ref, (idx, scale) in zip(o_refs[len(splits):len(splits) + n_c], copies):
        c_ref[...] = (segs[idx] * scale).astype(bf16)
    for t_ref, idx in zip(o_refs[len(splits) + n_c:], transposed):
        t_ref[...] = segs[idx].T


def _proj(x, w_bf16, splits, tile_m, copies=(), transposed=(), seq_len=None):
    t, d = x.shape
    n = w_bf16.shape[1]
    assert t % tile_m == 0 and sum(splits) == n
    widths = list(splits) + [splits[idx] for idx, _ in copies]
    dtypes = [f32] * len(splits) + [bf16] * len(copies)
    return pl.pallas_call(
        functools.partial(_proj_kernel, splits=splits, copies=copies, transposed=transposed),
        grid=(t // tile_m,),
        in_specs=[
            pl.BlockSpec((tile_m, d), lambda i: (i, 0)),
            pl.BlockSpec((d, n), lambda i: (0, 0), pipeline_mode=pl.Buffered(1)),
        ],
        out_specs=[pl.BlockSpec((tile_m, s), lambda i: (i, 0)) for s in widths]
        + [pl.BlockSpec((None, splits[idx], tile_m), lambda i: (i // (seq_len // tile_m), 0, i % (seq_len // tile_m)))
           for idx in transposed],
        out_shape=[jax.ShapeDtypeStruct((t, s), dt) for s, dt in zip(widths, dtypes)]
        + [jax.ShapeDtypeStruct((t // seq_len, splits[idx], seq_len), f32) for idx in transposed],
        compiler_params=pltpu.CompilerParams(
            dimension_semantics=("arbitrary",), vmem_limit_bytes=VMEM_LIMIT_BYTES),
        name="proj_in",
    )(x, w_bf16)


def _sb_prompt_kernel(bias_ref, q_ref, k_ref, v_ref, o_ref, *, tile):
    hp = pl.program_id(1)
    qi = pl.program_id(2)
    heads = q_ref.shape[-1] // SB_HEAD_DIM
    row = lax.broadcasted_iota(jnp.int32, (tile, tile), 0)
    col = lax.broadcasted_iota(jnp.int32, (tile, tile), 1)
    later = (row > col).astype(bf16)
    causal = col < row
    qh = [q_ref[:, hh * SB_HEAD_DIM:(hh + 1) * SB_HEAD_DIM] for hh in range(heads)]
    bias = [bias_ref[hp * heads + hh] * LOG2_E for hh in range(heads)]

    def key_tile(j, state, masked):
        start = pl.multiple_of(j * tile, tile)
        hs = range(heads)
        lanes = [slice(h * SB_HEAD_DIM, (h + 1) * SB_HEAD_DIM) for h in hs]
        kh = [k_ref[pl.ds(start, tile), lanes[h]] for h in hs]
        z2 = [lax.dot_general(qh[h], kh[h], (((1,), (1,)), ((), ())), preferred_element_type=f32) + bias[h]
              for h in hs]
        sp2 = [jnp.maximum(z2[h], 0.0) + jnp.log2(1.0 + jnp.exp2(-jnp.abs(z2[h]))) for h in hs]
        drop = [jnp.where(causal, sp2[h], 0.0) if masked else sp2[h] for h in hs]
        inner = [jnp.dot(drop[h].astype(bf16), later, preferred_element_type=f32) for h in hs]
        a = [jnp.exp2(z2[h] - (sp2[h] + state[h][0] + inner[h])) for h in hs]
        if masked:
            a = [jnp.where(causal, a[h], 0.0) for h in hs]
        vh = [v_ref[pl.ds(start, tile), lanes[h]] for h in hs]
        acc = [state[h][1] + jnp.dot(a[h].astype(bf16), vh[h], preferred_element_type=f32) for h in hs]
        carry = [state[h][0] + jnp.sum(drop[h], axis=1, keepdims=True) for h in hs]
        return tuple((carry[h], acc[h]) for h in hs)

    state = tuple((jnp.zeros((tile, 1), f32), jnp.zeros((tile, SB_HEAD_DIM), f32)) for _ in range(heads))
    state = key_tile(qi, state, True)
    state = lax.fori_loop(0, qi, lambda s, st: key_tile(qi - 1 - s, st, False), state)
    o_ref[...] = jnp.concatenate([acc for _, acc in state], axis=1)


def _sb_prompt(q, k, v, bias, tile=SB_TILE):
    b, s, w = q.shape
    assert s % tile == 0 and w % SB_STEP_WIDTH == 0
    blk = lambda bb, hp, qi: (bb, qi, hp)
    full = lambda bb, hp, qi: (bb, 0, hp)
    return pl.pallas_call(
        functools.partial(_sb_prompt_kernel, tile=tile),
        grid=(b, w // SB_STEP_WIDTH, s // tile),
        in_specs=[
            pl.BlockSpec(memory_space=pltpu.SMEM),
            pl.BlockSpec((None, tile, SB_STEP_WIDTH), blk),
            pl.BlockSpec((None, s, SB_STEP_WIDTH), full),
            pl.BlockSpec((None, s, SB_STEP_WIDTH), full),
        ],
        out_specs=pl.BlockSpec((None, tile, SB_STEP_WIDTH), blk),
        out_shape=jax.ShapeDtypeStruct((b, s, w), f32),
        compiler_params=pltpu.CompilerParams(
            dimension_semantics=("arbitrary", "arbitrary", "arbitrary"),
            vmem_limit_bytes=VMEM_LIMIT_BYTES),
        name="sb_prompt",
    )(bias, q, k, v)


SBS_PAGES_PER_STEP = 32


def _sb_sample_kernel(pt_ref, bias_ref, q_ref, *refs, n_slots):
    del pt_ref
    k_refs = refs[:n_slots]
    v_refs = refs[n_slots:2 * n_slots]
    o_ref = refs[2 * n_slots]
    qb_ref, acc_ref, carry_ref, z_ref = refs[2 * n_slots + 1:]
    step = pl.program_id(1)

    @pl.when(step == 0)
    def _():
        q = q_ref[...] * SB_SCALE
        qb_ref[...] = jnp.broadcast_to(q, qb_ref.shape)
        acc_ref[...] = jnp.zeros_like(acc_ref)
        carry_ref[...] = jnp.zeros_like(carry_ref)

    row = lax.broadcasted_iota(jnp.int32, (PAGE_SIZE, PAGE_SIZE), 0)
    col = lax.broadcasted_iota(jnp.int32, (PAGE_SIZE, PAGE_SIZE), 1)
    later = (row > col).astype(bf16)
    later2 = jnp.concatenate([later, later], axis=0)

    for i, k_ref in enumerate(k_refs):
        for h in range(SB_HEADS):
            zh = jnp.sum(k_ref[h] * qb_ref[h], axis=0, keepdims=True) + bias_ref[h]
            z_ref[pl.ds(i * SB_HEADS + h, 1), :] = zh
    z = z_ref[...]
    softplus = jnp.maximum(z, 0.0) + jnp.log(1.0 + jnp.exp(-jnp.abs(z)))
    log_keep = -softplus
    hi = log_keep.astype(bf16)
    lo = (log_keep - hi.astype(f32)).astype(bf16)
    inner = jnp.dot(jnp.concatenate([hi, lo], axis=1), later2, preferred_element_type=f32)
    total = jnp.sum(log_keep, axis=1, keepdims=True)
    base = z - softplus + inner
    carry = carry_ref[...]
    for i in range(n_slots):
        sl = slice(i * SB_HEADS, (i + 1) * SB_HEADS)
        w = jnp.exp(base[sl] + carry)
        for h in range(SB_HEADS):
            acc_ref[h] += v_refs[i][h] * w[h:h + 1, :]
        carry = carry + total[sl]
    carry_ref[...] = carry

    @pl.when(step == pl.num_programs(1) - 1)
    def _():
        o_ref[...] = jnp.sum(acc_ref[...], axis=2, keepdims=True)


def _sb_sample_past(q, bias, k_pool, v_pool, page_table):
    b, n_pages = page_table.shape
    n_slots = SBS_PAGES_PER_STEP
    assert n_pages % n_slots == 0
    page_block = (None, SB_HEADS, SB_HEAD_DIM, PAGE_SIZE)

    def page_spec(slot):
        return pl.BlockSpec(
            page_block, lambda bb, s, pt, slot=slot: (pt[bb, n_pages - 1 - (s * n_slots + slot)], 0, 0, 0))

    grid_spec = pltpu.PrefetchScalarGridSpec(
        num_scalar_prefetch=1,
        grid=(b, n_pages // n_slots),
        in_specs=[
            pl.BlockSpec(memory_space=pltpu.SMEM),
            pl.BlockSpec((None, SB_HEADS, SB_HEAD_DIM, 1), lambda bb, s, pt: (bb, 0, 0, 0)),
        ] + [page_spec(i) for i in range(n_slots)] * 2,
        out_specs=pl.BlockSpec((None, SB_HEADS, SB_HEAD_DIM, 1), lambda bb, s, pt: (bb, 0, 0, 0)),
        scratch_shapes=[
            pltpu.VMEM((SB_HEADS, SB_HEAD_DIM, PAGE_SIZE), f32),
            pltpu.VMEM((SB_HEADS, SB_HEAD_DIM, PAGE_SIZE), f32),
            pltpu.VMEM((SB_HEADS, 1), f32),
            pltpu.VMEM((n_slots * SB_HEADS, PAGE_SIZE), f32),
        ],
    )
    o = pl.pallas_call(
        functools.partial(_sb_sample_kernel, n_slots=n_slots),
        grid_spec=grid_spec,
        out_shape=jax.ShapeDtypeStruct((b, SB_HEADS, SB_HEAD_DIM, 1), f32),
        compiler_params=pltpu.CompilerParams(
            dimension_semantics=("arbitrary", "arbitrary"), vmem_limit_bytes=VMEM_LIMIT_BYTES),
        name="sb_sample",
    )(page_table, bias, q[..., None], *([k_pool] * n_slots), *([v_pool] * n_slots))
    return o[..., 0]


MAX_TOKEN_TILE = 512


def _token_tile(t):
    return max(m for m in range(SUBLANES, MAX_TOKEN_TILE + 1, SUBLANES) if t % m == 0)


def _route_kernel(h_ref, whi_ref, wlo_ref, b_ref, e_ref, p_ref, cnt_ref, hb_ref):
    h = h_ref[...]
    h_hi = h.astype(bf16)
    hb_ref[...] = h_hi
    h_lo = (h - h_hi.astype(f32)).astype(bf16)
    dot = lambda a, w_ref: jnp.dot(a, w_ref[...], preferred_element_type=f32)
    work = dot(h_hi, whi_ref) + (dot(h_hi, wlo_ref) + dot(h_lo, whi_ref)) + b_ref[...]
    lane = lax.broadcasted_iota(jnp.int32, work.shape, 1)
    picked = jnp.zeros(work.shape, jnp.bool_)
    top_e, top_l = [], []
    for _ in range(TOP_K):
        m = jnp.max(work, axis=1, keepdims=True)
        idx = jnp.min(jnp.where(work == m, lane, N_EXPERTS), axis=1, keepdims=True)
        hit = lane == idx
        picked = picked | hit
        work = jnp.where(hit, -jnp.inf, work)
        top_e.append(idx)
        top_l.append(m)
    ex = [jnp.exp(l - top_l[0]) for l in top_l]
    den = ex[0] + ex[1] + ex[2] + ex[3]
    e_ref[...] = jnp.concatenate(top_e, axis=1)
    p_ref[...] = jnp.concatenate([x / den for x in ex], axis=1)
    cnt_ref[...] = jnp.sum(picked.astype(f32), axis=0, keepdims=True).astype(jnp.int32)


def _route(h, w_router, b_router):
    t, d = h.shape
    tile = _token_tile(t)
    w_hi = w_router.astype(bf16)
    w_lo = (w_router - w_hi.astype(f32)).astype(bf16)
    const = lambda a: pl.BlockSpec(a.shape, lambda i: (0,) * a.ndim)
    b2 = b_router[None, :]
    return pl.pallas_call(
        _route_kernel,
        grid=(t // tile,),
        in_specs=[pl.BlockSpec((tile, d), lambda i: (i, 0)), const(w_hi), const(w_lo), const(b2)],
        out_specs=[pl.BlockSpec((tile, TOP_K), lambda i: (i, 0)), pl.BlockSpec((tile, TOP_K), lambda i: (i, 0)),
                   pl.BlockSpec((None, 1, N_EXPERTS), lambda i: (i, 0, 0)), pl.BlockSpec((tile, d), lambda i: (i, 0))],
        out_shape=[jax.ShapeDtypeStruct((t, TOP_K), jnp.int32), jax.ShapeDtypeStruct((t, TOP_K), f32),
                   jax.ShapeDtypeStruct((t // tile, 1, N_EXPERTS), jnp.int32), jax.ShapeDtypeStruct((t, d), bf16)],
        compiler_params=pltpu.CompilerParams(
            dimension_semantics=("arbitrary",), vmem_limit_bytes=VMEM_LIMIT_BYTES),
        name="moe_route",
    )(h, w_hi, w_lo, b2)


def _pair_pos_kernel(e_ref, base_ref, pos_ref):
    tile = e_ref.shape[0]
    lane = lax.broadcasted_iota(jnp.int32, (tile, N_EXPERTS), 1)
    hits = [lane == e_ref[:, j:j + 1] for j in range(TOP_K)]
    any_hit = hits[0] | hits[1] | hits[2] | hits[3]
    row = lax.broadcasted_iota(jnp.int32, (tile, tile), 0)
    col = lax.broadcasted_iota(jnp.int32, (tile, tile), 1)
    earlier = (col < row).astype(bf16)
    before = jnp.dot(earlier, any_hit.astype(bf16), preferred_element_type=f32) + base_ref[...].astype(f32)
    pos = [jnp.sum(jnp.where(hit, before, 0.0), axis=1, keepdims=True) for hit in hits]
    pos_ref[...] = jnp.concatenate(pos, axis=1).astype(jnp.int32)


def _pair_pos(top_e, tile_base):
    t = top_e.shape[0]
    n_tiles = tile_base.shape[0]
    tile = t // n_tiles
    return pl.pallas_call(
        _pair_pos_kernel,
        grid=(n_tiles,),
        in_specs=[pl.BlockSpec((tile, TOP_K), lambda i: (i, 0)),
                  pl.BlockSpec((None, 1, N_EXPERTS), lambda i: (i, 0, 0))],
        out_specs=pl.BlockSpec((tile, TOP_K), lambda i: (i, 0)),
        out_shape=jax.ShapeDtypeStruct((t, TOP_K), jnp.int32),
        compiler_params=pltpu.CompilerParams(
            dimension_semantics=("arbitrary",), vmem_limit_bytes=VMEM_LIMIT_BYTES),
        name="moe_pair_pos",
    )(top_e, tile_base)


def _moe_kernel(tile_ref, exp_ref, lo_ref, hi_ref, first_ref, x_ref, wgu_ref, bg_ref, bl_ref, wd_ref, bd_ref,
                o_ref, wg_s, wl_s, wd_s):
    del tile_ref
    i = pl.program_id(0)
    ff = wd_ref.shape[0]
    half = MXU_DIM // 2

    src = lax.broadcasted_iota(jnp.int32, (MXU_DIM, MXU_DIM), 0)
    dst = lax.broadcasted_iota(jnp.int32, (MXU_DIM, MXU_DIM), 1)
    unzip = (src == jnp.where(dst < half, 2 * dst, 2 * (dst - half) + 1)).astype(bf16)

    @pl.when((i == 0) | (exp_ref[i] != exp_ref[jnp.maximum(i - 1, 0)]))
    def _():
        for c in range(2 * ff // MXU_DIM):
            cols = slice(c * MXU_DIM, (c + 1) * MXU_DIM)
            out = slice(c * half, (c + 1) * half)
            w = jnp.dot(wgu_ref[:, cols].astype(bf16), unzip, preferred_element_type=f32).astype(bf16)
            wg_s[:, out] = w[:, :half]
            wl_s[:, out] = w[:, half:]
        wd_s[...] = wd_ref[...].astype(bf16)

    @pl.when(first_ref[i] == 1)
    def _():
        o_ref[...] = jnp.zeros_like(o_ref)

    @pl.when(hi_ref[i] > lo_ref[i])
    def _():
        x = x_ref[...]
        g = jnp.dot(x, wg_s[...], preferred_element_type=f32) + bg_ref[...]
        l = jnp.dot(x, wl_s[...], preferred_element_type=f32) + bl_ref[...]
        glu = jnp.minimum(g, SWIGLU_LIMIT)
        lin = jnp.clip(l, -SWIGLU_LIMIT, SWIGLU_LIMIT)
        h = glu * (1.0 / (1.0 + jnp.exp(-SWIGLU_ALPHA * glu))) * (lin + 1.0)
        y = jnp.dot(h.astype(bf16), wd_s[...], preferred_element_type=f32) + bd_ref[...]
        row = lax.broadcasted_iota(jnp.int32, (x.shape[0], 1), 0)
        mine = (row >= lo_ref[i]) & (row < hi_ref[i])
        o_ref[...] = jnp.where(mine, y.astype(o_ref.dtype), o_ref[...])


def _moe_experts(xs, items, w_gate_up, b_gate_up, w_down, b_down, tile_m):
    n_rows, d = xs.shape
    ff = w_down.shape[1]
    n_items = items[0].shape[0]
    row = lambda i, tl, ex, lo, hi, fi: (tl[i], 0)
    wsel = lambda i, tl, ex, lo, hi, fi: (ex[i], 0, 0)
    grid_spec = pltpu.PrefetchScalarGridSpec(
        num_scalar_prefetch=5,
        grid=(n_items,),
        in_specs=[
            pl.BlockSpec((tile_m, d), row),
            pl.BlockSpec((None, d, 2 * ff), wsel),
            pl.BlockSpec((None, 1, ff), wsel),
            pl.BlockSpec((None, 1, ff), wsel),
            pl.BlockSpec((None, ff, d), wsel),
            pl.BlockSpec((None, 1, d), wsel),
        ],
        out_specs=pl.BlockSpec((tile_m, d), row),
        scratch_shapes=[
            pltpu.VMEM((d, ff), bf16), pltpu.VMEM((d, ff), bf16), pltpu.VMEM((ff, d), bf16),
        ],
    )
    return pl.pallas_call(
        _moe_kernel,
        grid_spec=grid_spec,
        out_shape=jax.ShapeDtypeStruct((n_rows, d), bf16),
        compiler_params=pltpu.CompilerParams(
            dimension_semantics=("arbitrary",), vmem_limit_bytes=VMEM_LIMIT_BYTES),
        name="moe_experts",
    )(*items, xs, w_gate_up, b_gate_up[:, None, 0::2], b_gate_up[:, None, 1::2], w_down, b_down[:, None, :])


def _moe_work_items(starts, n_tiles, tile_m):
    counts = starts[1:] - starts[:-1]
    first_tile = starts[:-1] // tile_m
    n_t = jnp.where(counts > 0, (starts[1:] - 1) // tile_m - first_tile + 1, 0)
    item_end = jnp.cumsum(n_t)
    item_start = item_end - n_t
    n_items = n_tiles + N_EXPERTS - 1
    i = jnp.arange(n_items, dtype=jnp.int32)
    valid = i < item_end[-1]
    ii = jnp.minimum(i, item_end[-1] - 1)
    exp = jnp.sum((ii[:, None] >= item_end[None, :]).astype(jnp.int32), axis=1)
    onehot = (exp[:, None] == jnp.arange(N_EXPERTS, dtype=jnp.int32)[None, :]).astype(jnp.int32)
    pick = lambda table: jnp.sum(onehot * table[None, :], axis=1)
    tile = pick(first_tile) + ii - pick(item_start)
    lo = jnp.clip(pick(starts[:-1]) - tile * tile_m, 0, tile_m)
    hi = jnp.clip(pick(starts[1:]) - tile * tile_m, 0, tile_m)
    hi = jnp.where(valid, hi, lo)
    first = jnp.concatenate([jnp.ones((1,), jnp.int32), (tile[1:] != tile[:-1]).astype(jnp.int32)])
    return tuple(a.astype(jnp.int32) for a in (tile, exp, lo, hi, first))


def _moe(h, w_router, b_router, w_gate_up, b_gate_up, w_down, b_down, tile_m=MOE_TILE_M):
    t = h.shape[0]
    n_pairs = t * TOP_K
    n_tiles = -(-n_pairs // tile_m)
    top_e, top_p, tile_counts, h_bf16 = _route(h, w_router, b_router)
    counts = jnp.sum(tile_counts[:, 0, :], axis=0)
    starts = jnp.concatenate([jnp.zeros((1,), jnp.int32), jnp.cumsum(counts)]).astype(jnp.int32)
    tile_base = (jnp.cumsum(tile_counts, axis=0) - tile_counts + starts[None, None, :-1]).astype(jnp.int32)
    pos = _pair_pos(top_e, tile_base)
    bits = max(1, (n_pairs - 1).bit_length())
    assert N_EXPERTS << bits < 2 ** 31
    keys = (top_e.reshape(-1) << bits) | jnp.arange(n_pairs, dtype=jnp.int32)
    order = jnp.sort(keys) & ((1 << bits) - 1)
    row_tok = jnp.concatenate([order // TOP_K, jnp.zeros((n_tiles * tile_m - n_pairs,), jnp.int32)])
    tall = jnp.concatenate([h_bf16, jnp.zeros((n_tiles * tile_m - t + 8, h.shape[1]), bf16)], axis=0)
    tall, idx = lax.optimization_barrier((tall, row_tok))
    xs = lax.optimization_barrier(tall[idx])
    ys = _moe_experts(xs, _moe_work_items(starts, n_tiles, tile_m), w_gate_up, b_gate_up, w_down, b_down, tile_m)
    picked = [lax.optimization_barrier(ys[pos[:, j]]) for j in range(TOP_K)]
    return picked, top_p


RW_LANE_HALF = LANES // 2
RW_KJ = RW_HEAD_DIM // 2
RW_VG = RW_HEAD_DIM // 8
RW_STEPS_PER_BLOCK = 32


def _rwkv_scan_kernel(kk_ref, d_ref, kka_ref, km_ref, r_ref, v_ref, s0_ref, y_ref, s_ref, *, steps):
    tb = pl.program_id(1)

    @pl.when(tb == 0)
    def _():
        s_ref[...] = s0_ref[...]

    def fold(x):
        return x + pltpu.roll(x, RW_LANE_HALF, axis=1)

    def project(kk_rows):
        acc = [None] * RW_VG
        for j in range(RW_KJ):
            kkj = kk_rows(j)
            for g in range(RW_VG):
                p = s_ref[g, j] * kkj
                acc[g] = p if acc[g] is None else acc[g] + p
        return tuple(acc)

    def step(t, acc, t_next):
        sa = [-fold(a) for a in acc]
        vv = [v_ref[t, g] for g in range(RW_VG)]
        yacc = [None] * RW_VG
        nxt = [None] * RW_VG
        for j in range(RW_KJ):
            dj = d_ref[t, pl.ds(j, 1), :]
            kkaj = kka_ref[t, pl.ds(j, 1), :]
            kmj = km_ref[t, pl.ds(j, 1), :]
            rj = r_ref[t, pl.ds(j, 1), :]
            kkn = None if t_next is None else kk_ref[t_next, pl.ds(j, 1), :]
            for g in range(RW_VG):
                s = s_ref[g, j] * dj + sa[g] * kkaj + vv[g] * kmj
                s_ref[g, j] = s
                p = s * rj
                yacc[g] = p if yacc[g] is None else yacc[g] + p
                if kkn is not None:
                    q = s * kkn
                    nxt[g] = q if nxt[g] is None else nxt[g] + q
        for g in range(RW_VG):
            y_ref[t, g] = fold(yacc[g])
        return tuple(nxt)

    acc = project(lambda j: kk_ref[0, pl.ds(j, 1), :])
    acc = lax.fori_loop(0, steps - 1, lambda t, a: step(t, a, t + 1), acc)
    step(steps - 1, acc, None)


def _rwkv_scan(kk, d, kka, km, r, v, s0):
    n_g, s_len = kk.shape[:2]
    steps = min(RW_STEPS_PER_BLOCK, s_len)
    assert s_len % steps == 0
    krow = pl.BlockSpec((None, steps, RW_KJ, LANES), lambda g, t: (g, t, 0, 0))
    vrow = pl.BlockSpec((None, steps, RW_VG, 8, LANES), lambda g, t: (g, t, 0, 0, 0))
    st = pl.BlockSpec((None, RW_VG, RW_KJ, 8, LANES), lambda g, t: (g, 0, 0, 0, 0))
    return pl.pallas_call(
        functools.partial(_rwkv_scan_kernel, steps=steps),
        grid=(n_g, s_len // steps),
        in_specs=[krow, krow, krow, krow, krow, vrow, st],
        out_specs=[vrow, st],
        out_shape=[jax.ShapeDtypeStruct(v.shape, f32), jax.ShapeDtypeStruct(s0.shape, f32)],
        compiler_params=pltpu.CompilerParams(
            dimension_semantics=("arbitrary", "arbitrary"), vmem_limit_bytes=VMEM_LIMIT_BYTES),
        name="rwkv_scan",
    )(kk, d, kka, km, r, v, s0)


def _rw_groups(n_bh):
    assert n_bh % RW_LANE_HALF == 0
    return n_bh // RW_LANE_HALF


def _rw_k_layout(x):
    b, s, h, _ = x.shape
    g = _rw_groups(b * h)
    x = x.reshape(b, s, h, 2, RW_KJ)
    x = jnp.transpose(x, (1, 4, 3, 0, 2)).reshape(s, RW_KJ, 2, g, RW_LANE_HALF)
    return jnp.transpose(x, (3, 0, 1, 2, 4)).reshape(g, s, RW_KJ, LANES)


def _rw_v_layout(x):
    b, s, h, _ = x.shape
    g = _rw_groups(b * h)
    x = jnp.transpose(x, (1, 3, 0, 2)).reshape(s, RW_HEAD_DIM, g, RW_LANE_HALF)
    x = jnp.transpose(x, (2, 0, 1, 3))
    x = jnp.concatenate([x, x], axis=-1)
    return x.reshape(g, s, RW_VG, 8, LANES)


def _rw_v_unlayout(y, b, h):
    g, s = y.shape[:2]
    y = y.reshape(g, s, RW_HEAD_DIM, LANES)[..., :RW_LANE_HALF]
    y = jnp.transpose(y, (1, 2, 0, 3)).reshape(s, RW_HEAD_DIM, b, h)
    return jnp.transpose(y, (2, 0, 3, 1))


def _rw_state_layout(wkv):
    b, h = wkv.shape[:2]
    g = _rw_groups(b * h)
    x = wkv.reshape(g, RW_LANE_HALF, RW_VG, 8, 2, RW_KJ)
    return jnp.transpose(x, (0, 2, 5, 3, 4, 1)).reshape(g, RW_VG, RW_KJ, 8, LANES)


def _rw_state_unlayout(st, b, h):
    g = st.shape[0]
    x = st.reshape(g, RW_VG, RW_KJ, 8, 2, RW_LANE_HALF)
    x = jnp.transpose(x, (0, 5, 1, 3, 4, 2))
    return x.reshape(b, h, RW_HEAD_DIM, RW_HEAD_DIM)


MIX_TILE_M = 512


def _layer_norm_rows(pre, g, b):
    mu = jnp.mean(pre, axis=1, keepdims=True)
    cen = pre - mu
    var = jnp.mean(cen * cen, axis=1, keepdims=True)
    return cen * lax.rsqrt(var + LN_EPS) * g + b


def _sigmoid(x):
    return 1.0 / (1.0 + jnp.exp(-x))


def _mem_attend(q, mk_ref, mv_ref):
    heads = []
    for hd in range(MEM_HEADS):
        lanes = slice(hd * MEM_HEAD_DIM, (hd + 1) * MEM_HEAD_DIM)
        s = lax.dot_general(q[:, lanes], mk_ref[:, lanes].astype(bf16), (((1,), (1,)), ((), ())),
                            preferred_element_type=f32) * MEM_SCALE
        p = jnp.exp(s - jnp.max(s, axis=1, keepdims=True))
        p = p / jnp.sum(p, axis=1, keepdims=True)
        heads.append(jnp.dot(p.astype(bf16), mv_ref[:, lanes].astype(bf16), preferred_element_type=f32))
    return jnp.concatenate(heads, axis=1)


def _mem_attend_kernel(q_ref, mk_ref, mv_ref, o_ref):
    o_ref[...] = _mem_attend(q_ref[...].astype(bf16), mk_ref, mv_ref)


def _mem_attend_tokens(q_mem, mem_k, mem_v):
    b, s, w = q_mem.shape
    tok = pl.BlockSpec((None, s, w), lambda bb: (bb, 0, 0))
    mem = pl.BlockSpec((None, N_MEM, MEM_WIDTH), lambda bb: (bb, 0, 0))
    return pl.pallas_call(
        _mem_attend_kernel,
        grid=(b,),
        in_specs=[tok, mem, mem],
        out_specs=tok,
        out_shape=jax.ShapeDtypeStruct((b, s, w), f32),
        compiler_params=pltpu.CompilerParams(
            dimension_semantics=("arbitrary",), vmem_limit_bytes=VMEM_LIMIT_BYTES),
        name="mem_attend",
    )(q_mem, mem_k, mem_v)


def _mix_kernel(*refs, attend):
    if attend:
        x_ref, osb_ref, third_ref, orw_ref, gsb_ref, gmem_ref, grw_ref, mk_ref, mv_ref = refs[:9]
        o_mem = _mem_attend(third_ref[...].astype(bf16), mk_ref, mv_ref)
    else:
        x_ref, osb_ref, third_ref, orw_ref, gsb_ref, gmem_ref, grw_ref = refs[:7]
        o_mem = third_ref[...]
    wsb_ref, wmem_ref, wrw_ref, wout_ref, g1_ref, b1_ref, h_ref = refs[-7:]
    branch = lambda o, w_ref: jnp.dot(o.astype(bf16), w_ref[...], preferred_element_type=f32)
    merged = (_sigmoid(gsb_ref[...]) * branch(osb_ref[...], wsb_ref)
              + _sigmoid(gmem_ref[...]) * branch(o_mem, wmem_ref)
              + _sigmoid(grw_ref[...]) * branch(orw_ref[...], wrw_ref))
    pre = DN_ALPHA * x_ref[...] + branch(merged, wout_ref)
    h_ref[...] = _layer_norm_rows(pre, g1_ref[...], b1_ref[...])


def _mix(x, o_sb, q_or_o_mem, o_rw, g_sb, g_mem, g_rw, mem_kv, w_sb_o, w_mem_o, w_rw_o, w_out, ln_g, ln_b):
    b, s, d = x.shape
    tile = min(MIX_TILE_M, s)
    assert s % tile == 0
    tok = lambda w: pl.BlockSpec((None, tile, w), lambda bb, i: (bb, i, 0))
    mem = pl.BlockSpec((None, N_MEM, MEM_WIDTH), lambda bb, i: (bb, 0, 0))
    const = lambda a: pl.BlockSpec(a.shape, lambda bb, i: (0,) * a.ndim)
    weights = [w.astype(bf16) for w in (w_sb_o, w_mem_o, w_rw_o, w_out)] + [ln_g[None, :], ln_b[None, :]]
    mems = () if mem_kv is None else tuple(mem_kv)
    return pl.pallas_call(
        functools.partial(_mix_kernel, attend=mem_kv is not None),
        grid=(b, s // tile),
        in_specs=[tok(d), tok(SB_WIDTH), tok(MEM_WIDTH), tok(RW_WIDTH), tok(d), tok(d), tok(d)]
        + [mem] * len(mems) + [const(w) for w in weights],
        out_specs=tok(d),
        out_shape=jax.ShapeDtypeStruct((b, s, d), f32),
        compiler_params=pltpu.CompilerParams(
            dimension_semantics=("arbitrary", "arbitrary"), vmem_limit_bytes=VMEM_LIMIT_BYTES),
        name="mix_ln1",
    )(x, o_sb, q_or_o_mem, o_rw, g_sb, g_mem, g_rw, *mems, *weights)


def _combine_kernel(*refs):
    pick_refs = refs[:TOP_K]
    p_ref, h_ref, g_ref, b_ref, y_ref = refs[TOP_K:]
    p = p_ref[...]
    moe = pick_refs[0][...].astype(f32) * p[:, 0:1]
    for j in range(1, TOP_K):
        moe = moe + pick_refs[j][...].astype(f32) * p[:, j:j + 1]
    y_ref[...] = _layer_norm_rows(DN_ALPHA * h_ref[...] + moe, g_ref[...], b_ref[...])


def _combine(picked, top_p, h, ln_g, ln_b, first_row, n_rows):
    d = h.shape[1]
    tile = min(MIX_TILE_M, n_rows)
    assert n_rows % tile == 0 and first_row % tile == 0
    first = first_row // tile
    row = lambda w: pl.BlockSpec((tile, w), lambda i: (first + i, 0))
    vec = pl.BlockSpec((1, d), lambda i: (0, 0))
    return pl.pallas_call(
        _combine_kernel,
        grid=(n_rows // tile,),
        in_specs=[row(d)] * TOP_K + [row(TOP_K), row(d), vec, vec],
        out_specs=pl.BlockSpec((tile, d), lambda i: (i, 0)),
        out_shape=jax.ShapeDtypeStruct((n_rows, d), f32),
        compiler_params=pltpu.CompilerParams(
            dimension_semantics=("arbitrary",), vmem_limit_bytes=VMEM_LIMIT_BYTES),
        name="combine_ln2",
    )(*picked, top_p, h, ln_g[None, :], ln_b[None, :])


def _sb_logits(q, k, bias):
    z = jnp.einsum('bqhd,bkhd->bhqk', q, k, preferred_element_type=f32) * SB_SCALE
    return z + bias.astype(f32)[None, :, None, None]


def _sb_weights(z, causal):
    log_keep = jnp.where(causal, jax.nn.log_sigmoid(-z), 0.0)
    log_survive = lax.cumsum(log_keep, axis=z.ndim - 1, reverse=True) - log_keep
    return jnp.where(causal, jnp.exp(jax.nn.log_sigmoid(z) + log_survive), 0.0)


def _sb_sample(q, k_new, v_new, bias, k_pool, v_pool, page_table):
    b, s = q.shape[:2]
    assert s == 1
    to_token_minor = lambda pool: jnp.transpose(pool, (0, 2, 3, 1))
    o_past = _sb_sample_past(q[:, 0], bias, to_token_minor(k_pool), to_token_minor(v_pool), page_table)
    new_pos = jnp.arange(s)
    w_new = _sb_weights(_sb_logits(q, k_new, bias), new_pos[None, :] < new_pos[:, None])
    o_new = jnp.einsum('bhqk,bkhd->bqhd', w_new, v_new)
    return (o_past[:, None] + o_new).reshape(b, s, SB_WIDTH)


def _rwkv7(p_rw, shift0, wkv0, mu_rw, w_decay0, w_decay2, w_aaa0, w_aaa2, w_gate2,
           rw_k_k, rw_k_a, rw_r_k, rw_gn_g, rw_gn_b):
    b, s, _ = p_rw.shape
    prev = jnp.concatenate([shift0[:, None, :], p_rw[:, :-1]], axis=1)
    xs = p_rw + mu_rw * (prev - p_rw)
    r, k, v, w_lo, a_lo, g_lo = _split(xs, RW_SPLITS)
    w_raw = w_decay0 + jnp.tanh(w_lo) @ w_decay2
    decay = jnp.exp(-jnp.exp(-jax.nn.softplus(-w_raw) - 0.5))
    a = jax.nn.sigmoid(w_aaa0 + a_lo @ w_aaa2)
    g = jax.nn.sigmoid(g_lo) @ w_gate2
    heads = lambda t: t.reshape(b, s, RW_HEADS, RW_HEAD_DIM)
    kk = heads(k * rw_k_k)
    kk = kk / jnp.maximum(jnp.sqrt(jnp.sum(kk * kk, axis=-1, keepdims=True)), 1e-12)
    k_mod = k * (1.0 + (a - 1.0) * rw_k_a)
    rh, kh, vh, ah, dh = heads(r), heads(k_mod), heads(v), heads(a), heads(decay)

    y, wkv = _rwkv_scan(_rw_k_layout(kk), _rw_k_layout(dh), _rw_k_layout(kk * ah), _rw_k_layout(kh),
                        _rw_k_layout(rh), _rw_v_layout(vh), _rw_state_layout(wkv0))
    y = _rw_v_unlayout(y, b, RW_HEADS)
    wkv = _rw_state_unlayout(wkv, b, RW_HEADS)
    mu = y.mean(-1, keepdims=True)
    var = jnp.square(y - mu).mean(-1, keepdims=True)
    y = ((y - mu) * lax.rsqrt(var + RW_GN_EPS)).reshape(b, s, RW_WIDTH) * rw_gn_g + rw_gn_b
    bonus = jnp.sum(rh * kh * rw_r_k, axis=-1, keepdims=True) * vh
    y = (y + bonus.reshape(b, s, RW_WIDTH)) * g
    return y, p_rw[:, -1], wkv


def _branch_mix(x, parts, o_sb, mem_k, mem_v, shift0, wkv0, lw):
    q_mem, p_rw, g_sb, g_mem, g_rw = parts
    o_rw, shift, wkv = _rwkv7(p_rw, shift0, wkv0, lw['mu_rw'], lw['w_decay0'], lw['w_decay2'],
                              lw['w_aaa0'], lw['w_aaa2'], lw['w_gate2'], lw['rw_k_k'], lw['rw_k_a'],
                              lw['rw_r_k'], lw['rw_gn_g'], lw['rw_gn_b'])
    tail = (lw['w_sb_o'], lw['w_mem_o'], lw['w_rw_o'], lw['w_out'], lw['ln1_g'], lw['ln1_b'])
    if x.shape[1] == 1:
        rows = lambda a: a.reshape(1, a.shape[0], a.shape[2])
        o_mem = _mem_attend_tokens(q_mem, mem_k, mem_v)
        h = _mix(rows(x), rows(o_sb), rows(o_mem), rows(o_rw), rows(g_sb), rows(g_mem), rows(g_rw), None, *tail)
        h = h.reshape(x.shape)
    else:
        h = _mix(x, o_sb, q_mem, o_rw, g_sb, g_mem, g_rw, (mem_k, mem_v), *tail)
    return h, shift, wkv


def kernel(x_prompt, x_sample, cache_sb_k, cache_sb_v, cache_mem_k, cache_mem_v, state_rw_shift,
           state_rw_wkv, page_table, mem_prompt, w_in, sb_bias, mu_rw, w_decay0, w_decay2, w_aaa0,
           w_aaa2, w_gate2, rw_k_k, rw_k_a, rw_r_k, rw_gn_g, rw_gn_b, w_mem_kv, w_sb_o, w_mem_o,
           w_rw_o, w_out, ln1_g, ln1_b, w_router, b_router, w_gate_up, b_gate_up, w_down, b_down,
           ln2_g, ln2_b):
    assert w_in.shape[0] == DEPTH == 1
    l = 0
    lw = dict(mu_rw=mu_rw[l], w_decay0=w_decay0[l], w_decay2=w_decay2[l], w_aaa0=w_aaa0[l],
              w_aaa2=w_aaa2[l], w_gate2=w_gate2[l], rw_k_k=rw_k_k[l], rw_k_a=rw_k_a[l],
              rw_r_k=rw_r_k[l], rw_gn_g=rw_gn_g[l], rw_gn_b=rw_gn_b[l], w_sb_o=w_sb_o[l],
              w_mem_o=w_mem_o[l], w_rw_o=w_rw_o[l], w_out=w_out[l], ln1_g=ln1_g[l], ln1_b=ln1_b[l])
    bp, sp, _ = x_prompt.shape
    bs, ss, _ = x_sample.shape
    w_in_b = w_in[l].astype(bf16)

    xp = x_prompt.reshape(bp * sp, D_MODEL)
    sb_operands = ((0, SB_SCALE * LOG2_E), (1, 1.0), (2, 1.0))
    *parts, q_op, k_op, v_op, k_t, v_t = _proj(xp, w_in_b, IN_SPLITS, PROJ_TILE_M, sb_operands, (1, 2), sp)
    rest = parts[3:]
    tok3 = lambda a: a.reshape(bp, sp, a.shape[-1])
    token_major = lambda a: jnp.transpose(a.reshape(bp, SB_HEADS, SB_HEAD_DIM, sp), (0, 3, 1, 2))[None]
    k_sb, v_sb = token_major(k_t), token_major(v_t)
    o_sb = _sb_prompt(tok3(q_op), tok3(k_op), tok3(v_op), sb_bias[l])
    mem_k, mem_v = _proj(mem_prompt.reshape(bp * N_MEM, D_MODEL), w_mem_kv[l].astype(bf16),
                         (MEM_WIDTH, MEM_WIDTH), PROJ_TILE_M)
    mem_k = mem_k.reshape(bp, N_MEM, MEM_WIDTH)
    mem_v = mem_v.reshape(bp, N_MEM, MEM_WIDTH)
    h_p, shift_p, wkv_p = _branch_mix(
        x_prompt, [tok3(a) for a in rest], o_sb, mem_k, mem_v, jnp.zeros((bp, RW_SHIFT_WIDTH), f32),
        jnp.zeros((bp, RW_HEADS, RW_HEAD_DIM, RW_HEAD_DIM), f32), lw)

    xs = x_sample.reshape(bs * ss, D_MODEL)
    q_s, k_s, v_s, *rest_s = _proj(xs, w_in_b, IN_SPLITS, bs * ss)
    tok3s = lambda a: a.reshape(bs, ss, a.shape[-1])
    heads_s = lambda a: a.reshape(bs, ss, SB_HEADS, SB_HEAD_DIM)
    o_sb_s = _sb_sample(heads_s(q_s), heads_s(k_s), heads_s(v_s), sb_bias[l], cache_sb_k[l], cache_sb_v[l],
                        page_table)
    h_s, shift_s, wkv_s = _branch_mix(
        x_sample, [tok3s(a) for a in rest_s], o_sb_s, cache_mem_k[l].reshape(bs, N_MEM, MEM_WIDTH),
        cache_mem_v[l].reshape(bs, N_MEM, MEM_WIDTH), state_rw_shift[l], state_rw_wkv[l], lw)

    h_all = jnp.concatenate([h_p.reshape(bp * sp, D_MODEL), h_s.reshape(bs * ss, D_MODEL)], axis=0)
    picked, top_p = _moe(h_all, w_router[l], b_router[l], w_gate_up[l], b_gate_up[l], w_down[l], b_down[l])
    y_prompt = _combine(picked, top_p, h_all, ln2_g[l], ln2_b[l], 0, bp * sp).reshape(bp, sp, D_MODEL)
    y_sample = _combine(picked, top_p, h_all, ln2_g[l], ln2_b[l], bp * sp, bs * ss).reshape(bs, ss, D_MODEL)

    heads_m = lambda a: a.reshape(1, bp, N_MEM, MEM_HEADS, MEM_HEAD_DIM)
    return (y_prompt, y_sample,
            k_sb, v_sb,
            heads_m(mem_k), heads_m(mem_v), shift_p[None], wkv_p[None],
            k_s.reshape(1, bs, ss, SB_HEADS, SB_HEAD_DIM), v_s.reshape(1, bs, ss, SB_HEADS, SB_HEAD_DIM),
            shift_s[None], wkv_s[None])
```
